```python
import jax
import jax.numpy as jnp
from jax import lax
import numpy as np

D_MODEL = 2048
BATCH = 1
SEQ = 16384
DEPTH = 4

GRID_W = 64
CTX_LEN = 256
HEAD_DIM = 128
BLK = 128
MIX_W = D_MODEL // 2
A_HEADS = MIX_W // HEAD_DIM
A_KV_HEADS = A_HEADS // 4
B_GROUP_DIM = 128
B_GROUPS = MIX_W // B_GROUP_DIM
C_HEADS = MIX_W // HEAD_DIM
C_KV_HEADS = C_HEADS // 4
WINDOW = 128
D_HEAD = 64
D_HEADS = MIX_W // D_HEAD
D_DECAY_LORA = 64
D_ICLR_LORA = 64
D_GATE_LORA = 160
D_FF = 5632
ROPE_THETA = 10000.0
EPS = 1e-6
GN_EPS = 64e-5
N_EVEN = (DEPTH + 1) // 2
N_ODD = DEPTH // 2
EVEN_IN = (A_HEADS + 2 * A_KV_HEADS) * HEAD_DIM + 2 * MIX_W
C_IN = (C_HEADS + 2 * C_KV_HEADS) * HEAD_DIM
D_IN = 3 * MIX_W + D_GATE_LORA + 2 * D_DECAY_LORA + 2 * D_ICLR_LORA
ODD_IN = C_IN + D_IN

kernel_name = 'hybrid_prefix_dit_gqa_sgu_swa_rwkv7'


def rmsnorm(x, g):
    xf = x.astype(jnp.float32)
    y = xf * lax.rsqrt(jnp.mean(xf * xf, axis=-1, keepdims=True) + EPS)
    return (y * g.astype(jnp.float32)).astype(x.dtype)


def modulate(h, shift, scale):
    return h * (1 + scale) + shift


def axial_rope(rows):
    half = HEAD_DIM // 2
    inv = ROPE_THETA ** (-jnp.arange(0, half, 2, dtype=jnp.float32) / half)
    row = jnp.repeat(jnp.arange(rows, dtype=jnp.float32), GRID_W)
    col = jnp.tile(jnp.arange(GRID_W, dtype=jnp.float32), rows)
    ang_r = row[:, None] * inv[None, :]
    ang_c = col[:, None] * inv[None, :]
    ang = jnp.concatenate([ang_r, ang_r, ang_c, ang_c], axis=-1)
    return jnp.cos(ang), jnp.sin(ang)


def apply_rope(x, cos, sin):
    x1, x2, x3, x4 = jnp.split(x, 4, axis=-1)
    rot = jnp.concatenate([-x2, x1, -x4, x3], axis=-1)
    out = x.astype(jnp.float32) * cos[None, :, None, :] + rot.astype(jnp.float32) * sin[None, :, None, :]
    return out.astype(x.dtype)


def dwconv3(z, w):
    return lax.conv_general_dilated(z, w[:, None, :].astype(z.dtype), window_strides=(1,), padding=((1, 1),),
                                    dimension_numbers=('NWC', 'WIO', 'NWC'), feature_group_count=z.shape[-1])


def dense_gqa(q, k, v, sink=None):
    bsz, tq, hq, dh = q.shape
    hkv = k.shape[2]
    grp = hq // hkv
    scale = dh ** -0.5
    qb = q.reshape(bsz, tq // BLK, BLK, hkv, grp, dh).transpose(1, 0, 2, 3, 4, 5)

    def one_block(qblk):
        s = jnp.einsum('bqhgd,bkhd->bhgqk', qblk, k).astype(jnp.float32) * scale
        if sink is not None:
            s_sink = jnp.broadcast_to(sink.reshape(hkv, grp, 1, 1).astype(jnp.float32), s.shape[:-1] + (1,))
            p = jax.nn.softmax(jnp.concatenate([s, s_sink], axis=-1), axis=-1)[..., :-1]
        else:
            p = jax.nn.softmax(s, axis=-1)
        return jnp.einsum('bhgqk,bkhd->bqhgd', p.astype(v.dtype), v)

    o = lax.map(one_block, qb)
    return o.transpose(1, 0, 2, 3, 4, 5).reshape(bsz, tq, hq, dh)


def window_gqa(q, k, v, kc, vc, sink):
    bsz, t, hq, dh = q.shape
    hkv = k.shape[2]
    grp = hq // hkv
    nb = t // BLK
    nband = 3 * BLK
    scale = dh ** -0.5
    qb = q.reshape(bsz, nb, BLK, hkv, grp, dh).transpose(1, 0, 2, 3, 4, 5)
    pad = ((0, 0), (BLK, BLK), (0, 0), (0, 0))
    kp = jnp.pad(k, pad)
    vp = jnp.pad(v, pad)
    rel = jnp.arange(nband)[None, :] - BLK - jnp.arange(BLK)[:, None]
    sink_f = sink.reshape(hkv, grp, 1, 1).astype(jnp.float32)

    def one_block(args):
        qblk, n = args
        kb = lax.dynamic_slice_in_dim(kp, n * BLK, nband, axis=1)
        vb = lax.dynamic_slice_in_dim(vp, n * BLK, nband, axis=1)
        kpos = (n - 1) * BLK + jnp.arange(nband)
        valid = (jnp.abs(rel) <= WINDOW) & (kpos >= 0)[None, :] & (kpos < t)[None, :]
        s_loc = jnp.einsum('bqhgd,bkhd->bhgqk', qblk, kb).astype(jnp.float32) * scale
        s_loc = jnp.where(valid, s_loc, -jnp.inf)
        s_ctx = jnp.einsum('bqhgd,bkhd->bhgqk', qblk, kc).astype(jnp.float32) * scale
        s_sink = jnp.broadcast_to(sink_f, s_loc.shape[:-1] + (1,))
        p = jax.nn.softmax(jnp.concatenate([s_loc, s_ctx, s_sink], axis=-1), axis=-1).astype(v.dtype)
        return (jnp.einsum('bhgqk,bkhd->bqhgd', p[..., :nband], vb)
                + jnp.einsum('bhgqk,bkhd->bqhgd', p[..., nband:-1], vc))

    o = lax.map(one_block, (qb, jnp.arange(nb)))
    return o.transpose(1, 0, 2, 3, 4, 5).reshape(bsz, t, hq, dh)


def chunk_sgu(u, v, norm_g, w_s, b_s):
    bsz, t, _ = u.shape
    shp = (bsz, t // BLK, BLK, B_GROUPS, B_GROUP_DIM)
    vf = v.reshape(shp).astype(jnp.float32)
    mu = jnp.mean(vf, axis=-1, keepdims=True)
    var = jnp.mean(jnp.square(vf - mu), axis=-1, keepdims=True)
    vn = ((vf - mu) * lax.rsqrt(var + EPS)).astype(v.dtype) * norm_g.reshape(B_GROUPS, B_GROUP_DIM)
    vm = jnp.einsum('gij,bnjgc->bnigc', w_s, vn) + b_s.T[None, None, :, :, None]
    return (u.reshape(shp) * vm).reshape(bsz, t, MIX_W)


def even_layer_mixer(hx, hc, cos, sin, w_in, w_out, q_norm_g, k_norm_g, v_norm_g, w_s, b_s, with_ctx):
    o_k = A_HEADS * HEAD_DIM
    o_v = o_k + A_KV_HEADS * HEAD_DIM
    o_u = o_v + A_KV_HEADS * HEAD_DIM
    o_z = o_u + MIX_W

    def project(h):
        bsz, t, _ = h.shape
        p = h @ w_in
        q = rmsnorm(p[..., :o_k].reshape(bsz, t, A_HEADS, HEAD_DIM), q_norm_g)
        k = rmsnorm(p[..., o_k:o_v].reshape(bsz, t, A_KV_HEADS, HEAD_DIM), k_norm_g)
        v = p[..., o_v:o_u].reshape(bsz, t, A_KV_HEADS, HEAD_DIM)
        return q, k, v, jax.nn.gelu(p[..., o_u:o_z]), jax.nn.gelu(p[..., o_z:])

    bsz, t, _ = hx.shape
    qx, kx, vx, ux, zx = project(hx)
    qc, kc, vc, uc, zc = project(hc)
    qx = apply_rope(qx, cos, sin)
    kx = apply_rope(kx, cos, sin)
    k_all = jnp.concatenate([kx, kc], axis=1)
    v_all = jnp.concatenate([vx, vc], axis=1)
    ax = dense_gqa(qx, k_all, v_all).reshape(bsz, t, MIX_W)
    bx = chunk_sgu(ux, zx, v_norm_g, w_s, b_s)
    out_x = jnp.concatenate([ax, bx], axis=-1) @ w_out
    if not with_ctx:
        return out_x, None
    ac = dense_gqa(qc, kc, vc).reshape(bsz, hc.shape[1], MIX_W)
    bc = chunk_sgu(uc, zc, v_norm_g, w_s, b_s)
    return out_x, jnp.concatenate([ac, bc], axis=-1) @ w_out


def token_shift(z, mu):
    w = jnp.stack([mu[0], 1 - mu[0] - mu[1], mu[1]], axis=0)
    return dwconv3(z, w)


def rwkv7_features(z, w0, w2, a0, a2, g2, k_k, k_a):
    bsz, t, _ = z.shape
    o1, o2, o3 = MIX_W, 2 * MIX_W, 3 * MIX_W
    o4 = o3 + D_GATE_LORA
    o5 = o4 + 2 * D_DECAY_LORA
    r, k, v = z[..., :o1], z[..., o1:o2], z[..., o2:o3]
    g = jax.nn.sigmoid(z[..., o3:o4]) @ g2
    w_lo = jnp.tanh(z[..., o4:o5].reshape(bsz, t, 2, D_DECAY_LORA))
    a_lo = z[..., o5:].reshape(bsz, t, 2, D_ICLR_LORA)
    logw = -jax.nn.softplus(-(w0 + jnp.einsum('btdr,drc->btdc', w_lo, w2)).astype(jnp.float32)) - 0.5
    decay = jnp.exp(-jnp.exp(logw))
    a = jax.nn.sigmoid((a0 + jnp.einsum('btdr,drc->btdc', a_lo, a2)).astype(jnp.float32))
    kf = k.astype(jnp.float32)
    kk = (kf * k_k).reshape(bsz, t, D_HEADS, D_HEAD)
    kk = kk / jnp.maximum(jnp.sqrt(jnp.sum(kk * kk, axis=-1, keepdims=True)), 1e-12)
    k_dir = kf[:, :, None, :] * (1 + (a - 1) * k_a)
    heads = lambda u: u.reshape(bsz, t, D_HEADS, D_HEAD)
    dheads = lambda u: u.reshape(bsz, t, 2, D_HEADS, D_HEAD)
    feats = (heads(r.astype(jnp.float32)), heads(v.astype(jnp.float32)), dheads(decay), dheads(k_dir), kk, dheads(a))
    return feats, g


def wkv7_scan(state, w, k, v, a, b, r=None):
    seq = (w, k, v, a, b) if r is None else (w, k, v, a, b, r)
    xs = tuple(jnp.swapaxes(u, 0, 1) for u in seq)

    def step(S, inp):
        w_t, k_t, v_t, a_t, b_t = inp[:5]
        sa = jnp.einsum('bhij,bhj->bhi', S, a_t)
        S = S * w_t[:, :, None, :] + v_t[..., :, None] * k_t[..., None, :] + sa[..., :, None] * b_t[..., None, :]
        if r is None:
            return S, None
        return S, jnp.einsum('bhij,bhj->bhi', S, inp[5])

    S, ys = lax.scan(step, state, xs)
    return S, (None if r is None else jnp.swapaxes(ys, 0, 1))


def rwkv7_direction(feats, d, state, emit):
    r, v, decay, k_dir, kk, a = feats
    seq = (decay[:, :, d], k_dir[:, :, d], v, -kk, kk * a[:, :, d], r)
    if d == 1:
        seq = tuple(u[:, ::-1] for u in seq)
    state, y = wkv7_scan(state, *seq[:5], r=(seq[5] if emit else None))
    if emit and d == 1:
        y = y[:, ::-1]
    return state, y


def rwkv7_readout(y, feats, g, r_k, ln_w, ln_b, dtype):
    r, v, _, k_dir, _, _ = feats
    bsz, t = y.shape[:2]
    mu = jnp.mean(y, axis=-1, keepdims=True)
    var = jnp.mean(jnp.square(y - mu), axis=-1, keepdims=True)
    yn = ((y - mu) * lax.rsqrt(var + GN_EPS)).reshape(bsz, t, MIX_W) * ln_w + ln_b
    bonus = jnp.einsum('bthn,btdhn,hn->bth', r, k_dir, r_k)[..., None] * v
    return ((yn + bonus.reshape(bsz, t, MIX_W)) * g).astype(dtype)


def odd_layer_mixer(hx, hc, cos, sin, w_in, w_out, sink, shift_mu, w0, w2, a0, a2, g2, k_k, k_a, r_k,
                    ln_w, ln_b, with_ctx):
    o_k = C_HEADS * HEAD_DIM
    o_v = o_k + C_KV_HEADS * HEAD_DIM

    def attn_split(p):
        bsz, t, _ = p.shape
        return (p[..., :o_k].reshape(bsz, t, C_HEADS, HEAD_DIM),
                p[..., o_k:o_v].reshape(bsz, t, C_KV_HEADS, HEAD_DIM),
                p[..., o_v:C_IN].reshape(bsz, t, C_KV_HEADS, HEAD_DIM))

    bsz, t, _ = hx.shape
    tc = hc.shape[1]
    px = hx @ w_in
    pc = hc @ w_in
    qx, kx, vx = attn_split(px)
    qc, kc, vc = attn_split(pc)
    qx = apply_rope(qx, cos, sin)
    kx = apply_rope(kx, cos, sin)
    cx = window_gqa(qx, kx, vx, kc, vc, sink).reshape(bsz, t, MIX_W)
    fx, gx = rwkv7_features(token_shift(px[..., C_IN:], shift_mu), w0, w2, a0, a2, g2, k_k, k_a)
    fc, gc = rwkv7_features(token_shift(pc[..., C_IN:], shift_mu), w0, w2, a0, a2, g2, k_k, k_a)
    s0 = jnp.zeros((bsz, D_HEADS, D_HEAD, D_HEAD), jnp.float32)
    s_cf, y_cf = rwkv7_direction(fc, 0, s0, with_ctx)
    s_cb, y_cb = rwkv7_direction(fc, 1, s0, with_ctx)
    _, y_xf = rwkv7_direction(fx, 0, s_cf, True)
    _, y_xb = rwkv7_direction(fx, 1, s_cb, True)
    dx = rwkv7_readout(y_xf + y_xb, fx, gx, r_k, ln_w, ln_b, hx.dtype)
    out_x = jnp.concatenate([cx, dx], axis=-1) @ w_out
    if not with_ctx:
        return out_x, None
    cc = dense_gqa(qc, kc, vc, sink).reshape(bsz, tc, MIX_W)
    dc = rwkv7_readout(y_cf + y_cb, fc, gc, r_k, ln_w, ln_b, hc.dtype)
    return out_x, jnp.concatenate([cc, dc], axis=-1) @ w_out


def conv_ffn(h, w_in, conv_w, conv_b, w_out):
    up = dwconv3(h @ w_in, conv_w) + conv_b
    gate, val = jnp.split(up, 2, axis=-1)
    return (jax.nn.silu(gate) * val) @ w_out


def setup_inputs(seed: int = 0) -> dict:
    key = jax.random.key(seed)
    ks = iter(jax.random.split(key, 40))
    nrm = lambda shape, s: jax.random.normal(next(ks), shape, jnp.float32) * s
    gain = lambda shape: 1.0 + nrm(shape, 0.02)
    D, F = D_MODEL, D_FF
    return {
        'x': nrm((BATCH, SEQ, D), 1.0),
        'c': nrm((BATCH, D), 1.0),
        'ctx': nrm((BATCH, CTX_LEN, D), 1.0),
        'c_ctx': nrm((D,), 1.0),
        'ada_w': nrm((DEPTH, D, 6 * D), 0.5 * D ** -0.5),
        'ada_b': nrm((DEPTH, 6 * D), 0.02),
        'norm1_g': gain((DEPTH, D)),
        'norm2_g': gain((DEPTH, D)),
        'ffn_w_in': nrm((DEPTH, D, 2 * F), D ** -0.5),
        'ffn_conv_w': nrm((DEPTH, 3, 2 * F), 3 ** -0.5),
        'ffn_conv_b': nrm((DEPTH, 2 * F), 0.02),
        'ffn_w_out': nrm((DEPTH, F, D), F ** -0.5),
        'ev_w_in': nrm((N_EVEN, D, EVEN_IN), D ** -0.5),
        'ev_w_out': nrm((N_EVEN, 2 * MIX_W, D), (2 * MIX_W) ** -0.5),
        'a_q_norm_g': gain((N_EVEN, HEAD_DIM)),
        'a_k_norm_g': gain((N_EVEN, HEAD_DIM)),
        'b_v_norm_g': gain((N_EVEN, MIX_W)),
        'b_spatial_w': nrm((N_EVEN, B_GROUPS, BLK, BLK), BLK ** -0.5),
        'b_spatial_b': gain((N_EVEN, B_GROUPS, BLK)),
        'od_w_in': nrm((N_ODD, D, ODD_IN), D ** -0.5),
        'od_w_out': nrm((N_ODD, 2 * MIX_W, D), (2 * MIX_W) ** -0.5),
        'c_sink': nrm((N_ODD, C_HEADS), 0.5),
        'd_shift_mu': jax.random.uniform(next(ks), (N_ODD, 2, D_IN), jnp.float32, 0.0, 0.5),
        'd_w0': -2.0 + nrm((N_ODD, 2, MIX_W), 0.5),
        'd_w2': nrm((N_ODD, 2, D_DECAY_LORA, MIX_W), 0.5 * D_DECAY_LORA ** -0.5),
        'd_a0': nrm((N_ODD, 2, MIX_W), 0.1),
        'd_a2': nrm((N_ODD, 2, D_ICLR_LORA, MIX_W), 0.5 * D_ICLR_LORA ** -0.5),
        'd_g2': nrm((N_ODD, D_GATE_LORA, MIX_W), D_GATE_LORA ** -0.5),
        'd_k_k': 1.0 + nrm((N_ODD, MIX_W), 0.1),
        'd_k_a': gain((N_ODD, MIX_W)),
        'd_r_k': nrm((N_ODD, D_HEADS, D_HEAD), 0.1),
        'd_ln_w': gain((N_ODD, MIX_W)),
        'd_ln_b': nrm((N_ODD, MIX_W), 0.02),
        'final_norm_g': gain((D,)),
    }


def reference(x, c, ctx, c_ctx, ada_w, ada_b, norm1_g, norm2_g, ffn_w_in, ffn_conv_w, ffn_conv_b, ffn_w_out,
              ev_w_in, ev_w_out, a_q_norm_g, a_k_norm_g, b_v_norm_g, b_spatial_w, b_spatial_b,
              od_w_in, od_w_out, c_sink, d_shift_mu, d_w0, d_w2, d_a0, d_a2, d_g2, d_k_k, d_k_a, d_r_k,
              d_ln_w, d_ln_b, final_norm_g):
    rows = x.shape[1] // GRID_W
    cos, sin = axial_rope(rows)
    silu_c = jax.nn.silu(c)
    silu_cc = jax.nn.silu(c_ctx)
    for l in range(DEPTH):
        with_ctx = l < DEPTH - 1
        mod_x = (silu_c @ ada_w[l] + ada_b[l])[:, None, :]
        mod_c = silu_cc @ ada_w[l] + ada_b[l]
        sh1, sc1, g1, sh2, sc2, g2 = jnp.split(mod_x, 6, axis=-1)
        sh1c, sc1c, g1c, sh2c, sc2c, g2c = jnp.split(mod_c, 6, axis=-1)
        hx = modulate(rmsnorm(x, norm1_g[l]), sh1, sc1)
        hc = modulate(rmsnorm(ctx, norm1_g[l]), sh1c, sc1c)
        i = l // 2
        if l % 2 == 0:
            mix_x, mix_c = even_layer_mixer(hx, hc, cos, sin, ev_w_in[i], ev_w_out[i], a_q_norm_g[i],
                                            a_k_norm_g[i], b_v_norm_g[i], b_spatial_w[i], b_spatial_b[i],
                                            with_ctx)
        else:
            mix_x, mix_c = odd_layer_mixer(hx, hc, cos, sin, od_w_in[i], od_w_out[i], c_sink[i], d_shift_mu[i],
                                           d_w0[i], d_w2[i], d_a0[i], d_a2[i], d_g2[i], d_k_k[i], d_k_a[i],
                                           d_r_k[i], d_ln_w[i], d_ln_b[i], with_ctx)
        x = x + g1 * mix_x
        x = x + g2 * conv_ffn(modulate(rmsnorm(x, norm2_g[l]), sh2, sc2),
                              ffn_w_in[l], ffn_conv_w[l], ffn_conv_b[l], ffn_w_out[l])
        if with_ctx:
            ctx = ctx + g1c * mix_c
            ctx = ctx + g2c * conv_ffn(modulate(rmsnorm(ctx, norm2_g[l]), sh2c, sc2c),
                                       ffn_w_in[l], ffn_conv_w[l], ffn_conv_b[l], ffn_w_out[l])
    return rmsnorm(x, final_norm_g)
```

```python
import functools
import math

import jax
import jax.numpy as jnp
from jax import lax
from jax.experimental import pallas as pl
from jax.experimental.pallas import tpu as pltpu

F32 = jnp.float32
BF16 = jnp.bfloat16
HI = lax.Precision.HIGHEST

D_MODEL = 2048
GRID_W = 64
HEAD_DIM = 128
BLK = 128
MIX_W = D_MODEL // 2
N_Q_HEADS = MIX_W // HEAD_DIM
N_KV_HEADS = N_Q_HEADS // 4
GQA_GROUP = N_Q_HEADS // N_KV_HEADS
KV_W = N_KV_HEADS * HEAD_DIM
SGU_GROUPS = MIX_W // 128
RWKV_HEAD = 64
RWKV_PAIRS = MIX_W // (2 * RWKV_HEAD)
D_GATE_LORA = 160
D_GATE_PAD = 256
D_LORA = 64
D_FF = 5632
ROPE_THETA = 10000.0
EPS = 1e-6
GN_EPS = 64e-5
ATTN_SCALE = HEAD_DIM ** -0.5

HALO = 16
SCAN_CHUNK = 64
VMEM_LIMIT = 56 * 1024 * 1024

EV_Q, EV_U, EV_Z, EV_K, EV_V = 0, 1024, 2048, 3072, 3328
EV_N = 3584
OD_Q, OD_R, OD_K, OD_V, OD_AK, OD_AV, OD_G, OD_W, OD_A = 0, 1024, 2048, 3072, 4096, 4352, 4608, 4864, 4992
OD_N = 5120


def _cparams(*sem):
    return pltpu.CompilerParams(dimension_semantics=sem, vmem_limit_bytes=VMEM_LIMIT)


def _sigmoid(x):
    return 1.0 / (1.0 + jnp.exp(-x))


def _gelu_tanh(x):
    return 0.5 * x * (1.0 + jnp.tanh(math.sqrt(2.0 / math.pi) * (x + 0.044715 * (x * x * x))))


def _norm_mod(xv, g, sh, sc):
    ms = jnp.mean(xv * xv, axis=-1, keepdims=True)
    y = xv * lax.rsqrt(ms + EPS) * g
    return y * (1.0 + sc) + sh


def _ada_kernel(sv_ref, w_ref, b_ref, o_ref):
    kc = 256
    nk = w_ref.shape[1] // kc
    tn = w_ref.shape[2]

    def body(c, acc):
        a0, a1 = acc
        k0 = pl.multiple_of(c * kc, kc)
        w = w_ref[0, pl.ds(k0, kc), :]
        s = sv_ref[pl.ds(k0, kc), :]
        a0 = a0 + jnp.sum(w * s[:, 0:1], axis=0, keepdims=True)
        a1 = a1 + jnp.sum(w * s[:, 1:2], axis=0, keepdims=True)
        return a0, a1

    z = jnp.zeros((1, tn), F32)
    a0, a1 = lax.fori_loop(0, nk, body, (z, z))
    b = b_ref[0]
    o_ref[0] = jnp.concatenate([a0 + b, a1 + b], axis=0)


def _ada_mod(sv, ada_w, ada_b):
    depth, d, n = ada_w.shape
    tn = 1024
    return pl.pallas_call(
        _ada_kernel,
        grid=(depth, n // tn),
        in_specs=[pl.BlockSpec((d, 2), lambda l, j: (0, 0)),
                  pl.BlockSpec((1, d, tn), lambda l, j: (l, 0, j)),
                  pl.BlockSpec((1, 1, tn), lambda l, j: (l, 0, j))],
        out_specs=pl.BlockSpec((1, 2, tn), lambda l, j: (l, 0, j)),
        out_shape=jax.ShapeDtypeStruct((depth, 2, n), F32),
        compiler_params=_cparams("parallel", "parallel"),
        name="ada_mod",
    )(sv, ada_w, ada_b.reshape(depth, 1, n))


def _fill_hn(hn_ref, x_ref, xp_ref, xn_ref, g_ref, sh_ref, sc_ref, tm, first, last):
    g, sh, sc = g_ref[...], sh_ref[...], sc_ref[...]
    rc = 128
    for r in range(tm // rc):
        hn_ref[HALO + r * rc:HALO + (r + 1) * rc, :] = _norm_mod(x_ref[r * rc:(r + 1) * rc, :], g, sh, sc).astype(BF16)
    hp = _norm_mod(xp_ref[...], g, sh, sc)
    hn_ref[0:HALO, :] = jnp.where(first, 0.0, hp).astype(BF16)
    hx = _norm_mod(xn_ref[...], g, sh, sc)
    hn_ref[HALO + tm:2 * HALO + tm, :] = jnp.where(last, 0.0, hx).astype(BF16)


def _conv3(z, cw, tm):
    rows = z.shape[0]
    zp = pltpu.roll(z, 1, 0)
    zn = pltpu.roll(z, rows - 1, 0)
    out = cw[0:1, :] * zp + cw[1:2, :] * z + cw[2:3, :] * zn
    return out[HALO:HALO + tm, :]


def _proj_kernel(x_ref, g_ref, sh_ref, sc_ref, w_ref, o_ref, hn_ref, *, tm):
    @pl.when(pl.program_id(1) == 0)
    def _():
        g, sh, sc = g_ref[...], sh_ref[...], sc_ref[...]
        rc = 128
        for r in range(tm // rc):
            hn_ref[r * rc:(r + 1) * rc, :] = _norm_mod(x_ref[r * rc:(r + 1) * rc, :], g, sh, sc).astype(BF16)

    o_ref[...] = jnp.dot(hn_ref[...], w_ref[...], preferred_element_type=F32)


def _proj_shift_kernel(x_ref, xp_ref, xn_ref, g_ref, sh_ref, sc_ref, w_ref, cw_ref, o_ref, hn_ref, *, tm):
    i = pl.program_id(0)

    @pl.when(pl.program_id(1) == 0)
    def _():
        _fill_hn(hn_ref, x_ref, xp_ref, xn_ref, g_ref, sh_ref, sc_ref, tm, i == 0, i == pl.num_programs(0) - 1)

    z = jnp.dot(hn_ref[...], w_ref[...], preferred_element_type=F32)
    o_ref[...] = _conv3(z, cw_ref[...], tm)


def _halo_specs(t, tm, d):
    per = tm // HALO
    nblk = t // HALO
    return [pl.BlockSpec((tm, d), lambda i, j: (i, 0)),
            pl.BlockSpec((HALO, d), lambda i, j: (jnp.maximum(i * per - 1, 0), 0)),
            pl.BlockSpec((HALO, d), lambda i, j: (jnp.minimum((i + 1) * per, nblk - 1), 0))]


def _vec_spec(d):
    return pl.BlockSpec((1, d), lambda i, j: (0, 0))


def _pick_tm(t):
    return 512 if t % 512 == 0 else 256


def _proj(x, g, sh, sc, w, conv_w=None):
    t, d = x.shape
    n = w.shape[1]
    tm, tn = _pick_tm(t), 512
    grid = (t // tm, n // tn)
    wspec = pl.BlockSpec((d, tn), lambda i, j: (0, j))
    ospec = pl.BlockSpec((tm, tn), lambda i, j: (i, j))
    oshape = jax.ShapeDtypeStruct((t, n), F32)
    if conv_w is None:
        return pl.pallas_call(
            functools.partial(_proj_kernel, tm=tm), grid=grid,
            in_specs=[pl.BlockSpec((tm, d), lambda i, j: (i, 0)), _vec_spec(d), _vec_spec(d), _vec_spec(d), wspec],
            out_specs=ospec, out_shape=oshape,
            scratch_shapes=[pltpu.VMEM((tm, d), BF16)],
            compiler_params=_cparams("parallel", "arbitrary"), name="proj",
        )(x, g, sh, sc, w)
    return pl.pallas_call(
        functools.partial(_proj_shift_kernel, tm=tm), grid=grid,
        in_specs=_halo_specs(t, tm, d) + [_vec_spec(d), _vec_spec(d), _vec_spec(d), wspec,
                                         pl.BlockSpec((3, tn), lambda i, j: (0, j))],
        out_specs=ospec, out_shape=oshape,
        scratch_shapes=[pltpu.VMEM((tm + 2 * HALO, d), BF16)],
        compiler_params=_cparams("parallel", "arbitrary"), name="proj_shift",
    )(x, x, x, g, sh, sc, w, conv_w)


def _ffn_kernel(x_ref, xp_ref, xn_ref, g_ref, sh_ref, sc_ref, gate_ref, wg_ref, wv_ref, cwg_ref, cwv_ref,
                cbg_ref, cbv_ref, wo_ref, o_ref, hn_ref, acc_ref, *, tm):
    i, j = pl.program_id(0), pl.program_id(1)

    @pl.when(j == 0)
    def _():
        _fill_hn(hn_ref, x_ref, xp_ref, xn_ref, g_ref, sh_ref, sc_ref, tm, i == 0, i == pl.num_programs(0) - 1)
        acc_ref[...] = jnp.zeros_like(acc_ref)

    hn = hn_ref[...]
    gate = _conv3(jnp.dot(hn, wg_ref[...], preferred_element_type=F32), cwg_ref[...], tm) + cbg_ref[...]
    val = _conv3(jnp.dot(hn, wv_ref[...], preferred_element_type=F32), cwv_ref[...], tm) + cbv_ref[...]
    act = (gate * _sigmoid(gate) * val).astype(BF16)
    acc_ref[...] += jnp.dot(act, wo_ref[...], preferred_element_type=F32)

    @pl.when(j == pl.num_programs(1) - 1)
    def _():
        o_ref[...] = x_ref[...] + gate_ref[...] * acc_ref[...]


def _ffn(x, g, sh, sc, gate, w_in, conv_w, conv_b, w_out):
    t, d = x.shape
    f = w_out.shape[0]
    tm, tn = _pick_tm(t), 512
    nf = f // tn
    return pl.pallas_call(
        functools.partial(_ffn_kernel, tm=tm), grid=(t // tm, nf),
        in_specs=_halo_specs(t, tm, d) + [
            _vec_spec(d), _vec_spec(d), _vec_spec(d), _vec_spec(d),
            pl.BlockSpec((d, tn), lambda i, j: (0, j)),
            pl.BlockSpec((d, tn), lambda i, j: (0, j + nf)),
            pl.BlockSpec((3, tn), lambda i, j: (0, j)),
            pl.BlockSpec((3, tn), lambda i, j: (0, j + nf)),
            pl.BlockSpec((1, tn), lambda i, j: (0, j)),
            pl.BlockSpec((1, tn), lambda i, j: (0, j + nf)),
            pl.BlockSpec((tn, d), lambda i, j: (j, 0))],
        out_specs=pl.BlockSpec((tm, d), lambda i, j: (i, 0)),
        out_shape=jax.ShapeDtypeStruct((t, d), F32),
        scratch_shapes=[pltpu.VMEM((tm + 2 * HALO, d), BF16), pltpu.VMEM((tm, d), F32)],
        compiler_params=_cparams("parallel", "arbitrary"), name="conv_ffn",
    )(x, x, x, g, sh, sc, gate, w_in, w_in, conv_w, conv_w, conv_b, conv_b, w_out)


def _outproj_kernel(a_ref, b_ref, x_ref, gate_ref, wa_ref, wb_ref, o_ref):
    mix = jnp.dot(a_ref[...], wa_ref[...], preferred_element_type=F32)
    mix = mix + jnp.dot(b_ref[...], wb_ref[...], preferred_element_type=F32)
    o_ref[...] = x_ref[...] + gate_ref[...] * mix


def _outproj(a, b, x, gate, wa, wb):
    t, d = x.shape
    tm = _pick_tm(t)
    return pl.pallas_call(
        _outproj_kernel, grid=(t // tm,),
        in_specs=[pl.BlockSpec((tm, MIX_W), lambda i: (i, 0)),
                  pl.BlockSpec((tm, MIX_W), lambda i: (i, 0)),
                  pl.BlockSpec((tm, d), lambda i: (i, 0)),
                  pl.BlockSpec((1, d), lambda i: (0, 0)),
                  pl.BlockSpec((MIX_W, d), lambda i: (0, 0)),
                  pl.BlockSpec((MIX_W, d), lambda i: (0, 0))],
        out_specs=pl.BlockSpec((tm, d), lambda i: (i, 0)),
        out_shape=jax.ShapeDtypeStruct((t, d), F32),
        compiler_params=_cparams("parallel"), name="out_proj",
    )(a, b, x, gate, wa, wb)


def _rope(x, cos, sin):
    lane = lax.broadcasted_iota(jnp.int32, x.shape, 1)
    up = pltpu.roll(x, 32, 1)
    dn = pltpu.roll(x, HEAD_DIM - 32, 1)
    rot = jnp.where((lane % 64) < 32, -dn, up)
    return x * cos + rot * sin


def _qk_kernel(q_ref, k_ref, v_ref, cos_ref, sin_ref, qg_ref, kg_ref, qo_ref, ko_ref, vo_ref, *, norm):
    cos, sin = cos_ref[...], sin_ref[...]

    def prep(xh, g, scale):
        if norm:
            xh = xh * lax.rsqrt(jnp.mean(xh * xh, axis=-1, keepdims=True) + EPS) * g
        xh = _rope(xh, cos, sin)
        return (xh * scale).astype(BF16) if scale != 1.0 else xh.astype(BF16)

    for h in range(N_Q_HEADS):
        sl = slice(h * HEAD_DIM, (h + 1) * HEAD_DIM)
        qo_ref[:, sl] = prep(q_ref[:, sl], qg_ref[...], ATTN_SCALE)
    for h in range(N_KV_HEADS):
        sl = slice(h * HEAD_DIM, (h + 1) * HEAD_DIM)
        ko_ref[:, sl] = prep(k_ref[:, sl], kg_ref[...], 1.0)
    vo_ref[...] = v_ref[...].astype(BF16)


def _qk_prep(p, cos, sin, qg, kg, q_off, k_off, v_off, norm):
    t = p.shape[0]
    tm = 256
    return pl.pallas_call(
        functools.partial(_qk_kernel, norm=norm), grid=(t // tm,),
        in_specs=[pl.BlockSpec((tm, MIX_W), lambda i: (i, q_off // MIX_W)),
                  pl.BlockSpec((tm, KV_W), lambda i: (i, k_off // KV_W)),
                  pl.BlockSpec((tm, KV_W), lambda i: (i, v_off // KV_W)),
                  pl.BlockSpec((tm, HEAD_DIM), lambda i: (i, 0)),
                  pl.BlockSpec((tm, HEAD_DIM), lambda i: (i, 0)),
                  pl.BlockSpec((1, HEAD_DIM), lambda i: (0, 0)),
                  pl.BlockSpec((1, HEAD_DIM), lambda i: (0, 0))],
        out_specs=[pl.BlockSpec((tm, MIX_W), lambda i: (i, 0)),
                   pl.BlockSpec((tm, KV_W), lambda i: (i, 0)),
                   pl.BlockSpec((tm, KV_W), lambda i: (i, 0))],
        out_shape=[jax.ShapeDtypeStruct((t, MIX_W), BF16),
                   jax.ShapeDtypeStruct((t, KV_W), BF16),
                   jax.ShapeDtypeStruct((t, KV_W), BF16)],
        compiler_params=_cparams("parallel"), name="qk_prep",
    )(p, p, p, cos, sin, qg, kg)


def _flash_kernel(sink_ref, q_ref, k_ref, v_ref, o_ref, qs_ref, m_ref, l_ref, acc_ref, *, tq, use_sink):
    g, ki = pl.program_id(0), pl.program_id(2)

    @pl.when(ki == 0)
    def _():
        for h in range(GQA_GROUP):
            qs_ref[h * tq:(h + 1) * tq, :] = q_ref[:, h * HEAD_DIM:(h + 1) * HEAD_DIM]
            if use_sink:
                m_ref[h * tq:(h + 1) * tq, :] = jnp.full((tq, 1), sink_ref[g * GQA_GROUP + h], F32)
        if use_sink:
            l_ref[...] = jnp.ones_like(l_ref)
        else:
            m_ref[...] = jnp.full_like(m_ref, -jnp.inf)
            l_ref[...] = jnp.zeros_like(l_ref)
        acc_ref[...] = jnp.zeros_like(acc_ref)

    s = lax.dot_general(qs_ref[...], k_ref[...], (((1,), (1,)), ((), ())), preferred_element_type=F32)
    m_prev = m_ref[...]
    m_new = jnp.maximum(m_prev, jnp.max(s, axis=-1, keepdims=True))
    alpha = jnp.exp(m_prev - m_new)
    p = jnp.exp(s - m_new)
    l_ref[...] = alpha * l_ref[...] + jnp.sum(p, axis=-1, keepdims=True)
    acc_ref[...] = alpha * acc_ref[...] + jnp.dot(p.astype(BF16), v_ref[...], preferred_element_type=F32)
    m_ref[...] = m_new

    @pl.when(ki == pl.num_programs(2) - 1)
    def _():
        out = acc_ref[...] / l_ref[...]
        for h in range(GQA_GROUP):
            o_ref[:, h * HEAD_DIM:(h + 1) * HEAD_DIM] = out[h * tq:(h + 1) * tq, :].astype(BF16)


def _pick_tk(tk_total):
    for c in (1280, 1024, 512, 256):
        if tk_total % c == 0:
            return c
    return 128


def _dense_gqa(q, k, v, sink=None):
    tq_total, tk_total = q.shape[0], k.shape[0]
    tq = 256
    tk = _pick_tk(tk_total)
    use_sink = sink is not None
    sink_arr = sink if use_sink else jnp.zeros((N_Q_HEADS,), F32)
    gw = GQA_GROUP * HEAD_DIM
    return pl.pallas_call(
        functools.partial(_flash_kernel, tq=tq, use_sink=use_sink),
        grid=(N_KV_HEADS, tq_total // tq, tk_total // tk),
        in_specs=[pl.BlockSpec(memory_space=pltpu.SMEM),
                  pl.BlockSpec((tq, gw), lambda g, qi, ki: (qi, g)),
                  pl.BlockSpec((tk, HEAD_DIM), lambda g, qi, ki: (ki, g)),
                  pl.BlockSpec((tk, HEAD_DIM), lambda g, qi, ki: (ki, g))],
        out_specs=pl.BlockSpec((tq, gw), lambda g, qi, ki: (qi, g)),
        out_shape=jax.ShapeDtypeStruct((tq_total, MIX_W), BF16),
        scratch_shapes=[pltpu.VMEM((GQA_GROUP * tq, HEAD_DIM), BF16),
                        pltpu.VMEM((GQA_GROUP * tq, 1), F32),
                        pltpu.VMEM((GQA_GROUP * tq, 1), F32),
                        pltpu.VMEM((GQA_GROUP * tq, HEAD_DIM), F32)],
        compiler_params=_cparams("parallel", "parallel", "arbitrary"), name="dense_gqa",
    )(sink_arr, q, k, v)


def _window_kernel(sink_ref, q_ref, kp_ref, k0_ref, kx_ref, kc_ref, vp_ref, v0_ref, vx_ref, vc_ref, o_ref):
    g, n = pl.program_id(0), pl.program_id(1)
    nb = pl.num_programs(1)
    tc = kc_ref.shape[0]
    qs = jnp.concatenate([q_ref[:, h * HEAD_DIM:(h + 1) * HEAD_DIM] for h in range(GQA_GROUP)], axis=0)
    kcat = jnp.concatenate([kp_ref[...], k0_ref[...], kx_ref[...], kc_ref[...]], axis=0)
    vcat = jnp.concatenate([vp_ref[...], v0_ref[...], vx_ref[...], vc_ref[...]], axis=0)
    s = lax.dot_general(qs, kcat, (((1,), (1,)), ((), ())), preferred_element_type=F32)
    row = lax.broadcasted_iota(jnp.int32, s.shape, 0) % BLK
    col = lax.broadcasted_iota(jnp.int32, s.shape, 1)
    rel = col - BLK - row
    valid = (jnp.abs(rel) <= BLK) & ((col >= BLK) | (n > 0)) & ((col < 2 * BLK) | (n < nb - 1))
    valid = valid | (col >= 3 * BLK)
    s = jnp.where(valid, s, -jnp.inf)
    sink = jnp.concatenate([jnp.full((BLK, 1), sink_ref[g * GQA_GROUP + h], F32) for h in range(GQA_GROUP)], axis=0)
    m = jnp.maximum(jnp.max(s, axis=-1, keepdims=True), sink)
    p = jnp.exp(s - m)
    denom = jnp.sum(p, axis=-1, keepdims=True) + jnp.exp(sink - m)
    out = jnp.dot(p.astype(BF16), vcat, preferred_element_type=F32) / denom
    for h in range(GQA_GROUP):
        o_ref[:, h * HEAD_DIM:(h + 1) * HEAD_DIM] = out[h * BLK:(h + 1) * BLK, :].astype(BF16)


def _window_gqa(q, k, v, kc, vc, sink):
    t = q.shape[0]
    nb = t // BLK
    tc = kc.shape[0]
    gw = GQA_GROUP * HEAD_DIM
    blk = lambda f: pl.BlockSpec((BLK, HEAD_DIM), f)
    prev = lambda g, n: (jnp.maximum(n - 1, 0), g)
    cur = lambda g, n: (n, g)
    nxt = lambda g, n: (jnp.minimum(n + 1, nb - 1), g)
    ctx = pl.BlockSpec((tc, HEAD_DIM), lambda g, n: (0, g))
    return pl.pallas_call(
        _window_kernel, grid=(N_KV_HEADS, nb),
        in_specs=[pl.BlockSpec(memory_space=pltpu.SMEM),
                  pl.BlockSpec((BLK, gw), lambda g, n: (n, g)),
                  blk(prev), blk(cur), blk(nxt), ctx, blk(prev), blk(cur), blk(nxt), ctx],
        out_specs=pl.BlockSpec((BLK, gw), lambda g, n: (n, g)),
        out_shape=jax.ShapeDtypeStruct((t, MIX_W), BF16),
        compiler_params=_cparams("parallel", "parallel"), name="window_gqa",
    )(sink, q, k, k, k, kc, v, v, v, vc)


def _sgu_kernel(u_ref, z_ref, ng_ref, ws_ref, bs_ref, o_ref, *, chunks):
    for c in range(chunks):
        rows = slice(c * BLK, (c + 1) * BLK)
        for g in range(SGU_GROUPS):
            cols = slice(g * 128, (g + 1) * 128)
            z = _gelu_tanh(z_ref[rows, cols])
            mu = jnp.mean(z, axis=-1, keepdims=True)
            zc = z - mu
            var = jnp.mean(zc * zc, axis=-1, keepdims=True)
            vn = zc * lax.rsqrt(var + EPS) * ng_ref[:, cols]
            vm = jnp.dot(ws_ref[g], vn.astype(BF16), preferred_element_type=F32) + bs_ref[g]
            o_ref[rows, cols] = (_gelu_tanh(u_ref[rows, cols]) * vm).astype(BF16)


def _sgu(p, norm_g, w_s, b_s):
    t = p.shape[0]
    chunks = 2
    tm = chunks * BLK
    bs = jnp.broadcast_to(b_s[:, :, None], (SGU_GROUPS, BLK, 128))
    return pl.pallas_call(
        functools.partial(_sgu_kernel, chunks=chunks), grid=(t // tm,),
        in_specs=[pl.BlockSpec((tm, MIX_W), lambda i: (i, EV_U // MIX_W)),
                  pl.BlockSpec((tm, MIX_W), lambda i: (i, EV_Z // MIX_W)),
                  pl.BlockSpec((1, MIX_W), lambda i: (0, 0)),
                  pl.BlockSpec((SGU_GROUPS, BLK, BLK), lambda i: (0, 0, 0)),
                  pl.BlockSpec((SGU_GROUPS, BLK, 128), lambda i: (0, 0, 0))],
        out_specs=pl.BlockSpec((tm, MIX_W), lambda i: (i, 0)),
        out_shape=jax.ShapeDtypeStruct((t, MIX_W), BF16),
        compiler_params=_cparams("parallel"), name="chunk_sgu",
    )(p, p, norm_g.reshape(1, MIX_W), w_s.astype(BF16), bs)


def _seg_sum(x, e):
    parts = [jnp.dot(x[:, b * 128:(b + 1) * 128], e, precision=HI, preferred_element_type=F32)
             for b in range(x.shape[1] // 128)]
    return jnp.concatenate(parts, axis=1)


def _feat_kernel(r_ref, k_ref, v_ref, gl_ref, wl_ref, al_ref, g2_ref, w2_ref, a2_ref, w0_ref, a0_ref,
                 kk_ref, ka_ref, rk_ref, e_ref, ld_ref, kd_ref, ad_ref, kko_ref, bon_ref, go_ref):
    e = e_ref[...]
    r, k, v = r_ref[...], k_ref[...], v_ref[...]
    go_ref[...] = jnp.dot(_sigmoid(gl_ref[...]), g2_ref[...], preferred_element_type=F32)
    lw = w0_ref[...] + jnp.dot(jnp.tanh(wl_ref[...]), w2_ref[...], preferred_element_type=F32)
    nl = -lw
    softplus = jnp.maximum(nl, 0.0) + jnp.log(1.0 + jnp.exp(-jnp.abs(nl)))
    ld_ref[...] = -jnp.exp(-softplus - 0.5)
    a = _sigmoid(a0_ref[...] + jnp.dot(al_ref[...], a2_ref[...], preferred_element_type=F32))
    ad_ref[...] = a
    kk = k * kk_ref[...]
    nrm = jnp.maximum(jnp.sqrt(_seg_sum(kk * kk, e)), 1e-12)
    kko_ref[...] = kk / nrm
    ka = ka_ref[...]
    kd0 = k * (1.0 + (a[:, :MIX_W] - 1.0) * ka)
    kd1 = k * (1.0 + (a[:, MIX_W:] - 1.0) * ka)
    kd_ref[:, :MIX_W] = kd0
    kd_ref[:, MIX_W:] = kd1
    bon_ref[...] = _seg_sum(r * (kd0 + kd1) * rk_ref[...], e) * v


def _rwkv_features(zs, g2p, w2b, a2b, w0, a0, k_k, k_a, r_k, e):
    t = zs.shape[0]
    tm = 256
    col = lambda w, off: pl.BlockSpec((tm, w), lambda i: (i, off // w))
    full = lambda a: pl.BlockSpec(a.shape, lambda i: (0,) * a.ndim)
    wide = lambda w: pl.BlockSpec((tm, w), lambda i: (i, 0))
    params = [g2p, w2b, a2b, w0, a0, k_k, k_a, r_k, e]
    return pl.pallas_call(
        _feat_kernel, grid=(t // tm,),
        in_specs=[col(MIX_W, OD_R), col(MIX_W, OD_K), col(MIX_W, OD_V), col(D_GATE_PAD, OD_G),
                  col(128, OD_W), col(128, OD_A)] + [full(a) for a in params],
        out_specs=[wide(2 * MIX_W), wide(2 * MIX_W), wide(2 * MIX_W), wide(MIX_W), wide(MIX_W), wide(MIX_W)],
        out_shape=[jax.ShapeDtypeStruct((t, 2 * MIX_W), F32)] * 3 + [jax.ShapeDtypeStruct((t, MIX_W), F32)] * 3,
        compiler_params=_cparams("parallel"), name="rwkv_features",
    )(zs, zs, zs, zs, zs, zs, *params)


def _mm(a, b):
    return jnp.dot(a, b, precision=HI, preferred_element_type=F32)


def _mm_nt(a, b):
    return lax.dot_general(a, b, (((1,), (1,)), ((), ())), precision=HI, preferred_element_type=F32)


def _mm_tn(a, b):
    return lax.dot_general(a, b, (((0,), (0,)), ((), ())), precision=HI, preferred_element_type=F32)


def _scan_kernel(ld_ref, kd_ref, ad_ref, r_ref, v_ref, kk_ref, s0_ref, y_ref, sf_ref, s_ref, *, reverse):
    L = SCAN_CHUNK
    c = pl.program_id(0)

    @pl.when(c == 0)
    def _():
        s_ref[...] = s0_ref[...]

    ri = lax.broadcasted_iota(jnp.int32, (L, L), 0)
    ci = lax.broadcasted_iota(jnp.int32, (L, L), 1)
    tri = ((ci >= ri) if reverse else (ci <= ri)).astype(F32)
    r2 = lax.broadcasted_iota(jnp.int32, (2 * L, 2 * L), 0)
    c2 = lax.broadcasted_iota(jnp.int32, (2 * L, 2 * L), 1)
    same = (r2 // L) == (c2 // L)
    before = (c2 > r2) if reverse else (c2 < r2)
    strict = same & before
    incl = same & (before | (c2 == r2))
    lane = lax.broadcasted_iota(jnp.int32, (L, 2 * RWKV_HEAD), 1)
    m0 = (lane < RWKV_HEAD).astype(F32)
    m1 = 1.0 - m0
    last = 0 if reverse else L - 1

    for p in range(RWKV_PAIRS):
        cols = slice(p * 128, (p + 1) * 128)
        ld, kd, ad = ld_ref[:, cols], kd_ref[:, cols], ad_ref[:, cols]
        r, v, kk = r_ref[:, cols], v_ref[:, cols], kk_ref[:, cols]
        s_prev = s_ref[p]
        cinc = _mm(tri, ld)
        g_inc = jnp.exp(cinc)
        g_exc = jnp.exp(cinc - ld)
        g_inv = jnp.exp(-cinc)
        g_tot = g_inc[last:last + 1, :]
        at = -kk * g_exc
        rt = r * g_inc
        bt = kk * ad * g_inv
        kt = kd * g_inv
        la = jnp.concatenate([at * m0, at * m1], axis=0)
        lr = jnp.concatenate([rt * m0, rt * m1], axis=0)
        bb = jnp.concatenate([bt, bt], axis=0)
        kb = jnp.concatenate([kt, kt], axis=0)
        vbd = jnp.concatenate([v * m0, v * m1], axis=0)
        n_ab = jnp.where(strict, _mm_nt(la, bb), 0.0)
        m_ak = jnp.where(strict, _mm_nt(la, kb), 0.0)
        m_rb = jnp.where(incl, _mm_nt(lr, bb), 0.0)
        m_rk = jnp.where(incl, _mm_nt(lr, kb), 0.0)
        x = _mm_nt(la, s_prev) + _mm(m_ak, vbd)
        npow = n_ab
        steps = int(math.log2(L))
        for q in range(steps):
            x = x + _mm(npow, x)
            if q + 1 < steps:
                npow = _mm(npow, npow)
        ybd = _mm_nt(lr, s_prev) + _mm(m_rb, x) + _mm(m_rk, vbd)
        y_ref[:, cols] = ybd[:L, :] + ybd[L:, :]
        u = x[:L, :] + x[L:, :]
        upd = _mm_tn(jnp.concatenate([u, v], axis=0), jnp.concatenate([bt, kt], axis=0))
        s_new = (s_prev + jnp.where(same, upd, 0.0)) * g_tot
        s_ref[p] = s_new

    @pl.when(c == pl.num_programs(0) - 1)
    def _():
        sf_ref[...] = s_ref[...]


def _rwkv_scan(ld, kd, ad, zs, kk, s0, d):
    t = zs.shape[0]
    L = SCAN_CHUNK
    nc = t // L
    reverse = d == 1
    cidx = (lambda c: nc - 1 - c) if reverse else (lambda c: c)
    dirblk = pl.BlockSpec((L, MIX_W), lambda c: (cidx(c), d))
    sblk = pl.BlockSpec((RWKV_PAIRS, 128, 128), lambda c: (0, 0, 0))
    return pl.pallas_call(
        functools.partial(_scan_kernel, reverse=reverse), grid=(nc,),
        in_specs=[dirblk, dirblk, dirblk,
                  pl.BlockSpec((L, MIX_W), lambda c: (cidx(c), OD_R // MIX_W)),
                  pl.BlockSpec((L, MIX_W), lambda c: (cidx(c), OD_V // MIX_W)),
                  pl.BlockSpec((L, MIX_W), lambda c: (cidx(c), 0)),
                  sblk],
        out_specs=[pl.BlockSpec((L, MIX_W), lambda c: (cidx(c), 0)), sblk],
        out_shape=[jax.ShapeDtypeStruct((t, MIX_W), F32),
                   jax.ShapeDtypeStruct((RWKV_PAIRS, 128, 128), F32)],
        scratch_shapes=[pltpu.VMEM((RWKV_PAIRS, 128, 128), F32)],
        compiler_params=_cparams("arbitrary"), name="rwkv_scan_bwd" if reverse else "rwkv_scan_fwd",
    )(ld, kd, ad, zs, zs, kk, s0)


def _readout_kernel(yf_ref, yb_ref, bon_ref, g_ref, lw_ref, lb_ref, e_ref, o_ref):
    e = e_ref[...]
    y = yf_ref[...] + yb_ref[...]
    mu = _seg_sum(y, e) * (1.0 / RWKV_HEAD)
    yc = y - mu
    var = _seg_sum(yc * yc, e) * (1.0 / RWKV_HEAD)
    yn = yc * lax.rsqrt(var + GN_EPS) * lw_ref[...] + lb_ref[...]
    o_ref[...] = ((yn + bon_ref[...]) * g_ref[...]).astype(BF16)


def _rwkv_readout(yf, yb, bonus, g, ln_w, ln_b, e):
    t = yf.shape[0]
    tm = 256
    wide = pl.BlockSpec((tm, MIX_W), lambda i: (i, 0))
    vec = pl.BlockSpec((1, MIX_W), lambda i: (0, 0))
    return pl.pallas_call(
        _readout_kernel, grid=(t // tm,),
        in_specs=[wide, wide, wide, wide, vec, vec, pl.BlockSpec((128, 128), lambda i: (0, 0))],
        out_specs=wide, out_shape=jax.ShapeDtypeStruct((t, MIX_W), BF16),
        compiler_params=_cparams("parallel"), name="rwkv_readout",
    )(yf, yb, bonus, g, ln_w, ln_b, e)


def _final_norm_kernel(x_ref, g_ref, o_ref):
    xv = x_ref[...]
    o_ref[...] = xv * lax.rsqrt(jnp.mean(xv * xv, axis=-1, keepdims=True) + EPS) * g_ref[...]


def _final_norm(x, g):
    t, d = x.shape
    tm = 256
    return pl.pallas_call(
        _final_norm_kernel, grid=(t // tm,),
        in_specs=[pl.BlockSpec((tm, d), lambda i: (i, 0)), pl.BlockSpec((1, d), lambda i: (0, 0))],
        out_specs=pl.BlockSpec((tm, d), lambda i: (i, 0)),
        out_shape=jax.ShapeDtypeStruct((t, d), F32),
        compiler_params=_cparams("parallel"), name="final_norm",
    )(x, g)


def _rope_tables(rows):
    half = HEAD_DIM // 2
    inv = ROPE_THETA ** (-jnp.arange(0, half, 2, dtype=F32) / half)
    row = jnp.repeat(jnp.arange(rows, dtype=F32), GRID_W)
    col = jnp.tile(jnp.arange(GRID_W, dtype=F32), rows)
    ang_r = row[:, None] * inv[None, :]
    ang_c = col[:, None] * inv[None, :]
    ang = jnp.concatenate([ang_r, ang_r, ang_c, ang_c], axis=-1)
    return jnp.cos(ang), jnp.sin(ang)


def _even_layer(x, ctx, mx, mc, n1g, w_in, w_out, qg, kg, vng, w_s, b_s, tabs, with_ctx):
    o_k = N_Q_HEADS * HEAD_DIM
    o_v = o_k + KV_W
    o_u = o_v + KV_W
    o_z = o_u + MIX_W
    wp = jnp.concatenate([w_in[:, :o_k], w_in[:, o_u:o_z], w_in[:, o_z:], w_in[:, o_k:o_v], w_in[:, o_v:o_u]],
                         axis=1).astype(BF16)
    wa, wb = w_out[:MIX_W].astype(BF16), w_out[MIX_W:].astype(BF16)
    qg, kg = qg.reshape(1, HEAD_DIM), kg.reshape(1, HEAD_DIM)
    (cos_x, sin_x), (cos_c, sin_c) = tabs
    px = _proj(x, n1g, mx[0], mx[1], wp)
    pc = _proj(ctx, n1g, mc[0], mc[1], wp)
    qx, kx, vx = _qk_prep(px, cos_x, sin_x, qg, kg, EV_Q, EV_K, EV_V, True)
    qc, kc, vc = _qk_prep(pc, cos_c, sin_c, qg, kg, EV_Q, EV_K, EV_V, True)
    ax = _dense_gqa(qx, jnp.concatenate([kx, kc], axis=0), jnp.concatenate([vx, vc], axis=0))
    bx = _sgu(px, vng, w_s, b_s)
    x = _outproj(ax, bx, x, mx[2], wa, wb)
    if with_ctx:
        ac = _dense_gqa(qc, kc, vc)
        bc = _sgu(pc, vng, w_s, b_s)
        ctx = _outproj(ac, bc, ctx, mc[2], wa, wb)
    return x, ctx


def _odd_layer(x, ctx, mx, mc, n1g, w_in, w_out, sink, shift_mu, w0, w2, a0, a2, g2, k_k, k_a, r_k, ln_w, ln_b,
               tabs, with_ctx):
    o_k = N_Q_HEADS * HEAD_DIM
    o_v = o_k + KV_W
    c_in = o_v + KV_W
    o_g = 3 * MIX_W
    o_w = o_g + D_GATE_LORA
    o_a = o_w + 2 * D_LORA
    d = w_in.shape[0]
    wr = w_in[:, c_in:]
    gpad = jnp.zeros((d, D_GATE_PAD - D_GATE_LORA), w_in.dtype)
    wp = jnp.concatenate([w_in[:, :o_k], wr[:, :o_g], w_in[:, o_k:o_v], w_in[:, o_v:c_in],
                          wr[:, o_g:o_w], gpad, wr[:, o_w:o_a], wr[:, o_a:]], axis=1).astype(BF16)
    taps = jnp.stack([shift_mu[0], 1.0 - shift_mu[0] - shift_mu[1], shift_mu[1]], axis=0)
    ident = jnp.tile(jnp.array([[0.0], [1.0], [0.0]], F32), (1, 1))
    cw = jnp.concatenate([jnp.tile(ident, (1, MIX_W)), taps[:, :o_g], jnp.tile(ident, (1, 2 * KV_W)),
                          taps[:, o_g:o_w], jnp.tile(ident, (1, D_GATE_PAD - D_GATE_LORA)),
                          taps[:, o_w:o_a], taps[:, o_a:]], axis=1)
    wa, wb = w_out[:MIX_W].astype(BF16), w_out[MIX_W:].astype(BF16)
    g2p = jnp.concatenate([g2, jnp.zeros((D_GATE_PAD - D_GATE_LORA, MIX_W), g2.dtype)], axis=0)
    zl = jnp.zeros((D_LORA, MIX_W), F32)
    w2b = jnp.concatenate([jnp.concatenate([w2[0], zl], axis=1), jnp.concatenate([zl, w2[1]], axis=1)], axis=0)
    a2b = jnp.concatenate([jnp.concatenate([a2[0], zl], axis=1), jnp.concatenate([zl, a2[1]], axis=1)], axis=0)
    w0f, a0f = w0.reshape(1, 2 * MIX_W), a0.reshape(1, 2 * MIX_W)
    kkv, kav, rkv = k_k.reshape(1, MIX_W), k_a.reshape(1, MIX_W), r_k.reshape(1, MIX_W)
    lnw, lnb = ln_w.reshape(1, MIX_W), ln_b.reshape(1, MIX_W)
    lane = jnp.arange(128)
    e = (lane[:, None] // RWKV_HEAD == lane[None, :] // RWKV_HEAD).astype(F32)
    ones_g = jnp.ones((1, HEAD_DIM), F32)
    (cos_x, sin_x), (cos_c, sin_c) = tabs

    px = _proj(x, n1g, mx[0], mx[1], wp, cw)
    pc = _proj(ctx, n1g, mc[0], mc[1], wp, cw)
    qx, kx, vx = _qk_prep(px, cos_x, sin_x, ones_g, ones_g, OD_Q, OD_AK, OD_AV, False)
    qc, kc, vc = _qk_prep(pc, cos_c, sin_c, ones_g, ones_g, OD_Q, OD_AK, OD_AV, False)
    cx = _window_gqa(qx, kx, vx, kc, vc, sink)

    fparams = (g2p, w2b, a2b, w0f, a0f, kkv, kav, rkv, e)
    ld_x, kd_x, ad_x, kk_x, bon_x, g_x = _rwkv_features(px, *fparams)
    ld_c, kd_c, ad_c, kk_c, bon_c, g_c = _rwkv_features(pc, *fparams)
    s0 = jnp.zeros((RWKV_PAIRS, 128, 128), F32)
    y_cf, s_cf = _rwkv_scan(ld_c, kd_c, ad_c, pc, kk_c, s0, 0)
    y_cb, s_cb = _rwkv_scan(ld_c, kd_c, ad_c, pc, kk_c, s0, 1)
    y_xf, _ = _rwkv_scan(ld_x, kd_x, ad_x, px, kk_x, s_cf, 0)
    y_xb, _ = _rwkv_scan(ld_x, kd_x, ad_x, px, kk_x, s_cb, 1)
    dx = _rwkv_readout(y_xf, y_xb, bon_x, g_x, lnw, lnb, e)
    x = _outproj(cx, dx, x, mx[2], wa, wb)
    if with_ctx:
        cc = _dense_gqa(qc, kc, vc, sink)
        dc = _rwkv_readout(y_cf, y_cb, bon_c, g_c, lnw, lnb, e)
        ctx = _outproj(cc, dc, ctx, mc[2], wa, wb)
    return x, ctx


def kernel(x, c, ctx, c_ctx, ada_w, ada_b, norm1_g, norm2_g, ffn_w_in, ffn_conv_w, ffn_conv_b, ffn_w_out,
           ev_w_in, ev_w_out, a_q_norm_g, a_k_norm_g, b_v_norm_g, b_spatial_w, b_spatial_b,
           od_w_in, od_w_out, c_sink, d_shift_mu, d_w0, d_w2, d_a0, d_a2, d_g2, d_k_k, d_k_a, d_r_k,
           d_ln_w, d_ln_b, final_norm_g):
    bsz, t, d = x.shape
    assert bsz == 1 and d == D_MODEL
    tc = ctx.shape[1]
    depth = ada_w.shape[0]
    xs, cs = x[0], ctx[0]
    tabs = (_rope_tables(t // GRID_W), (jnp.ones((tc, HEAD_DIM), F32), jnp.zeros((tc, HEAD_DIM), F32)))
    sv = jnp.stack([jax.nn.silu(c[0]), jax.nn.silu(c_ctx)], axis=1)
    mods = _ada_mod(sv, ada_w, ada_b)
    for l in range(depth):
        with_ctx = l < depth - 1
        mx = [mods[l, 0, k * d:(k + 1) * d].reshape(1, d) for k in range(6)]
        mc = [mods[l, 1, k * d:(k + 1) * d].reshape(1, d) for k in range(6)]
        n1g, n2g = norm1_g[l].reshape(1, d), norm2_g[l].reshape(1, d)
        i = l // 2
        if l % 2 == 0:
            xs, cs = _even_layer(xs, cs, mx, mc, n1g, ev_w_in[i], ev_w_out[i], a_q_norm_g[i], a_k_norm_g[i],
                                 b_v_norm_g[i], b_spatial_w[i], b_spatial_b[i], tabs, with_ctx)
        else:
            xs, cs = _odd_layer(xs, cs, mx, mc, n1g, od_w_in[i], od_w_out[i], c_sink[i], d_shift_mu[i], d_w0[i],
                                d_w2[i], d_a0[i], d_a2[i], d_g2[i], d_k_k[i], d_k_a[i], d_r_k[i], d_ln_w[i],
                                d_ln_b[i], tabs, with_ctx)
        fw_in, fw_out = ffn_w_in[l].astype(BF16), ffn_w_out[l].astype(BF16)
        fcb = ffn_conv_b[l].reshape(1, 2 * D_FF)
        xs = _ffn(xs, n2g, mx[3], mx[4], mx[5], fw_in, ffn_conv_w[l], fcb, fw_out)
        if with_ctx:
            cs = _ffn(cs, n2g, mc[3], mc[4], mc[5], fw_in, ffn_conv_w[l], fcb, fw_out)
    return _final_norm(xs, final_norm_g.reshape(1, d))[None]
```

```python
import functools
import math

import jax
import jax.numpy as jnp
from jax import lax
from jax.experimental import pallas as pl
from jax.experimental.pallas import tpu as pltpu

F32 = jnp.float32
BF16 = jnp.bfloat16
HI = lax.Precision.HIGHEST

D_MODEL = 2048
GRID_W = 64
HEAD_DIM = 128
BLK = 128
MIX_W = D_MODEL // 2
N_Q_HEADS = MIX_W // HEAD_DIM
N_KV_HEADS = N_Q_HEADS // 4
GQA_GROUP = N_Q_HEADS // N_KV_HEADS
KV_W = N_KV_HEADS * HEAD_DIM
SGU_GROUPS = MIX_W // 128
RWKV_HEAD = 64
RWKV_PAIRS = MIX_W // (2 * RWKV_HEAD)
D_GATE_LORA = 160
D_GATE_PAD = 256
D_LORA = 64
D_FF = 5632
ROPE_THETA = 10000.0
EPS = 1e-6
GN_EPS = 64e-5
LOG2E = math.log2(math.e)
Q_SCALE = HEAD_DIM ** -0.5 * LOG2E
V_AUG = 2 * HEAD_DIM

HALO = 16
SCAN_CHUNK = 64
SCAN_GROUP_PAIRS = 4
VMEM_LIMIT = 56 * 1024 * 1024

EV_Q, EV_U, EV_Z, EV_K, EV_V = 0, 1024, 2048, 3072, 3328
EV_N = 3584
OD_Q, OD_R, OD_K, OD_V, OD_AK, OD_AV, OD_G, OD_W, OD_A = 0, 1024, 2048, 3072, 4096, 4352, 4608, 4864, 4992
OD_N = 5120


_NT = (((1,), (1,)), ((), ()))
_TN = (((0,), (0,)), ((), ()))


def _cparams(*sem):
    return pltpu.CompilerParams(dimension_semantics=sem, vmem_limit_bytes=VMEM_LIMIT)


def _sigmoid(x):
    return 1.0 / (1.0 + jnp.exp(-x))


def _gelu_tanh(x):
    return 0.5 * x * (1.0 + jnp.tanh(math.sqrt(2.0 / math.pi) * (x + 0.044715 * (x * x * x))))


def _norm_mod(xv, g, sh, sc):
    ms = jnp.mean(xv * xv, axis=-1, keepdims=True)
    y = xv * lax.rsqrt(ms + EPS) * g
    return y * (1.0 + sc) + sh


def _ada_kernel(sv_ref, w_ref, b_ref, o_ref):
    kc = 256
    nk = w_ref.shape[1] // kc
    tn = w_ref.shape[2]

    def body(c, acc):
        a0, a1 = acc
        k0 = pl.multiple_of(c * kc, kc)
        w = w_ref[0, pl.ds(k0, kc), :]
        s = sv_ref[pl.ds(k0, kc), :]
        a0 = a0 + jnp.sum(w * s[:, 0:1], axis=0, keepdims=True)
        a1 = a1 + jnp.sum(w * s[:, 1:2], axis=0, keepdims=True)
        return a0, a1

    z = jnp.zeros((1, tn), F32)
    a0, a1 = lax.fori_loop(0, nk, body, (z, z))
    b = b_ref[0]
    o_ref[0] = jnp.concatenate([a0 + b, a1 + b], axis=0)


def _ada_mod(sv, ada_w, ada_b):
    depth, d, n = ada_w.shape
    tn = 1024
    return pl.pallas_call(
        _ada_kernel,
        grid=(depth, n // tn),
        in_specs=[pl.BlockSpec((d, 2), lambda l, j: (0, 0)),
                  pl.BlockSpec((1, d, tn), lambda l, j: (l, 0, j)),
                  pl.BlockSpec((1, 1, tn), lambda l, j: (l, 0, j))],
        out_specs=pl.BlockSpec((1, 2, tn), lambda l, j: (l, 0, j)),
        out_shape=jax.ShapeDtypeStruct((depth, 2, n), F32),
        compiler_params=_cparams("parallel", "parallel"),
        name="ada_mod",
    )(sv, ada_w, ada_b.reshape(depth, 1, n))


def _fill_hn(hn_ref, x_ref, xp_ref, xn_ref, g_ref, sh_ref, sc_ref, tm, first, last):
    g, sh, sc = g_ref[...], sh_ref[...], sc_ref[...]
    rc = 128
    for r in range(tm // rc):
        hn_ref[HALO + r * rc:HALO + (r + 1) * rc, :] = _norm_mod(x_ref[r * rc:(r + 1) * rc, :], g, sh, sc).astype(BF16)
    hp = _norm_mod(xp_ref[...], g, sh, sc)
    hn_ref[0:HALO, :] = jnp.where(first, 0.0, hp).astype(BF16)
    hx = _norm_mod(xn_ref[...], g, sh, sc)
    hn_ref[HALO + tm:2 * HALO + tm, :] = jnp.where(last, 0.0, hx).astype(BF16)


def _conv3(z, cw, tm):
    rows = z.shape[0]
    zp = pltpu.roll(z, 1, 0)
    zn = pltpu.roll(z, rows - 1, 0)
    out = cw[0:1, :] * zp + cw[1:2, :] * z + cw[2:3, :] * zn
    return out[HALO:HALO + tm, :]


def _proj_kernel(x_ref, g_ref, sh_ref, sc_ref, w_ref, o_ref, hn_ref, *, tm):
    @pl.when(pl.program_id(1) == 0)
    def _():
        g, sh, sc = g_ref[...], sh_ref[...], sc_ref[...]
        rc = 128
        for r in range(tm // rc):
            hn_ref[r * rc:(r + 1) * rc, :] = _norm_mod(x_ref[r * rc:(r + 1) * rc, :], g, sh, sc).astype(BF16)

    o_ref[...] = jnp.dot(hn_ref[...], w_ref[...], preferred_element_type=F32)


def _proj_shift_kernel(x_ref, xp_ref, xn_ref, g_ref, sh_ref, sc_ref, w_ref, cw_ref, o_ref, hn_ref, *, tm):
    i = pl.program_id(0)

    @pl.when(pl.program_id(1) == 0)
    def _():
        _fill_hn(hn_ref, x_ref, xp_ref, xn_ref, g_ref, sh_ref, sc_ref, tm, i == 0, i == pl.num_programs(0) - 1)

    z = jnp.dot(hn_ref[...], w_ref[...], preferred_element_type=F32)
    o_ref[...] = _conv3(z, cw_ref[...], tm)


def _halo_specs(t, tm, d):
    per = tm // HALO
    nblk = t // HALO
    return [pl.BlockSpec((tm, d), lambda i, j: (i, 0)),
            pl.BlockSpec((HALO, d), lambda i, j: (jnp.maximum(i * per - 1, 0), 0)),
            pl.BlockSpec((HALO, d), lambda i, j: (jnp.minimum((i + 1) * per, nblk - 1), 0))]


def _vec_spec(d):
    return pl.BlockSpec((1, d), lambda i, j: (0, 0))


def _pick_tm(t):
    return 512 if t % 512 == 0 else 256


def _proj(x, g, sh, sc, w, conv_w=None):
    t, d = x.shape
    n = w.shape[1]
    tm, tn = _pick_tm(t), 512
    grid = (t // tm, n // tn)
    wspec = pl.BlockSpec((d, tn), lambda i, j: (0, j))
    ospec = pl.BlockSpec((tm, tn), lambda i, j: (i, j))
    oshape = jax.ShapeDtypeStruct((t, n), F32)
    if conv_w is None:
        return pl.pallas_call(
            functools.partial(_proj_kernel, tm=tm), grid=grid,
            in_specs=[pl.BlockSpec((tm, d), lambda i, j: (i, 0)), _vec_spec(d), _vec_spec(d), _vec_spec(d), wspec],
            out_specs=ospec, out_shape=oshape,
            scratch_shapes=[pltpu.VMEM((tm, d), BF16)],
            compiler_params=_cparams("parallel", "arbitrary"), name="proj",
        )(x, g, sh, sc, w)
    return pl.pallas_call(
        functools.partial(_proj_shift_kernel, tm=tm), grid=grid,
        in_specs=_halo_specs(t, tm, d) + [_vec_spec(d), _vec_spec(d), _vec_spec(d), wspec,
                                         pl.BlockSpec((3, tn), lambda i, j: (0, j))],
        out_specs=ospec, out_shape=oshape,
        scratch_shapes=[pltpu.VMEM((tm + 2 * HALO, d), BF16)],
        compiler_params=_cparams("parallel", "arbitrary"), name="proj_shift",
    )(x, x, x, g, sh, sc, w, conv_w)


def _ffn_kernel(x_ref, xp_ref, xn_ref, g_ref, sh_ref, sc_ref, gate_ref, wg_ref, wv_ref, cwg_ref, cwv_ref,
                cbg_ref, cbv_ref, wo_ref, o_ref, hn_ref, acc_ref, *, tm):
    i, j = pl.program_id(0), pl.program_id(1)

    @pl.when(j == 0)
    def _():
        _fill_hn(hn_ref, x_ref, xp_ref, xn_ref, g_ref, sh_ref, sc_ref, tm, i == 0, i == pl.num_programs(0) - 1)
        acc_ref[...] = jnp.zeros_like(acc_ref)

    hn = hn_ref[...]
    gate = _conv3(jnp.dot(hn, wg_ref[...], preferred_element_type=F32), cwg_ref[...], tm) + cbg_ref[...]
    val = _conv3(jnp.dot(hn, wv_ref[...], preferred_element_type=F32), cwv_ref[...], tm) + cbv_ref[...]
    act = (gate * _sigmoid(gate) * val).astype(BF16)
    acc_ref[...] += jnp.dot(act, wo_ref[...], preferred_element_type=F32)

    @pl.when(j == pl.num_programs(1) - 1)
    def _():
        o_ref[...] = x_ref[...] + gate_ref[...] * acc_ref[...]


def _ffn(x, g, sh, sc, gate, w_in, conv_w, conv_b, w_out):
    t, d = x.shape
    f = w_out.shape[0]
    tm, tn = _pick_tm(t), 512
    nf = f // tn
    return pl.pallas_call(
        functools.partial(_ffn_kernel, tm=tm), grid=(t // tm, nf),
        in_specs=_halo_specs(t, tm, d) + [
            _vec_spec(d), _vec_spec(d), _vec_spec(d), _vec_spec(d),
            pl.BlockSpec((d, tn), lambda i, j: (0, j)),
            pl.BlockSpec((d, tn), lambda i, j: (0, j + nf)),
            pl.BlockSpec((3, tn), lambda i, j: (0, j)),
            pl.BlockSpec((3, tn), lambda i, j: (0, j + nf)),
            pl.BlockSpec((1, tn), lambda i, j: (0, j)),
            pl.BlockSpec((1, tn), lambda i, j: (0, j + nf)),
            pl.BlockSpec((tn, d), lambda i, j: (j, 0))],
        out_specs=pl.BlockSpec((tm, d), lambda i, j: (i, 0)),
        out_shape=jax.ShapeDtypeStruct((t, d), F32),
        scratch_shapes=[pltpu.VMEM((tm + 2 * HALO, d), BF16), pltpu.VMEM((tm, d), F32)],
        compiler_params=_cparams("parallel", "arbitrary"), name="conv_ffn",
    )(x, x, x, g, sh, sc, gate, w_in, w_in, conv_w, conv_w, conv_b, conv_b, w_out)


def _outproj_kernel(a_ref, b_ref, x_ref, gate_ref, wa_ref, wb_ref, o_ref):
    mix = jnp.dot(a_ref[...], wa_ref[...], preferred_element_type=F32)
    mix = mix + jnp.dot(b_ref[...], wb_ref[...], preferred_element_type=F32)
    o_ref[...] = x_ref[...] + gate_ref[...] * mix


def _outproj(a, b, x, gate, wa, wb):
    t, d = x.shape
    tm = _pick_tm(t)
    return pl.pallas_call(
        _outproj_kernel, grid=(t // tm,),
        in_specs=[pl.BlockSpec((tm, MIX_W), lambda i: (i, 0)),
                  pl.BlockSpec((tm, MIX_W), lambda i: (i, 0)),
                  pl.BlockSpec((tm, d), lambda i: (i, 0)),
                  pl.BlockSpec((1, d), lambda i: (0, 0)),
                  pl.BlockSpec((MIX_W, d), lambda i: (0, 0)),
                  pl.BlockSpec((MIX_W, d), lambda i: (0, 0))],
        out_specs=pl.BlockSpec((tm, d), lambda i: (i, 0)),
        out_shape=jax.ShapeDtypeStruct((t, d), F32),
        compiler_params=_cparams("parallel"), name="out_proj",
    )(a, b, x, gate, wa, wb)


def _rope(x, cos, sin):
    lane = lax.broadcasted_iota(jnp.int32, x.shape, 1)
    up = pltpu.roll(x, 32, 1)
    dn = pltpu.roll(x, HEAD_DIM - 32, 1)
    rot = jnp.where((lane % 64) < 32, -dn, up)
    return x * cos + rot * sin


def _qk_kernel(q_ref, k_ref, v_ref, cos_ref, sin_ref, qg_ref, kg_ref, qo_ref, ko_ref, vo_ref, *, norm):
    cos, sin = cos_ref[...], sin_ref[...]

    def prep(xh, g, scale):
        if norm:
            xh = xh * lax.rsqrt(jnp.mean(xh * xh, axis=-1, keepdims=True) + EPS) * g
        xh = _rope(xh, cos, sin)
        return (xh * scale).astype(BF16) if scale != 1.0 else xh.astype(BF16)

    for h in range(N_Q_HEADS):
        sl = slice(h * HEAD_DIM, (h + 1) * HEAD_DIM)
        qo_ref[:, sl] = prep(q_ref[:, sl], qg_ref[...], Q_SCALE)
    for h in range(N_KV_HEADS):
        sl = slice(h * HEAD_DIM, (h + 1) * HEAD_DIM)
        ko_ref[:, sl] = prep(k_ref[:, sl], kg_ref[...], 1.0)
        vo_ref[:, h * V_AUG:h * V_AUG + HEAD_DIM] = v_ref[:, sl].astype(BF16)
        vo_ref[:, h * V_AUG + HEAD_DIM:(h + 1) * V_AUG] = jnp.ones((v_ref.shape[0], HEAD_DIM), BF16)


def _qk_prep(p, cos, sin, qg, kg, q_off, k_off, v_off, norm):
    t = p.shape[0]
    tm = 256
    return pl.pallas_call(
        functools.partial(_qk_kernel, norm=norm), grid=(t // tm,),
        in_specs=[pl.BlockSpec((tm, MIX_W), lambda i: (i, q_off // MIX_W)),
                  pl.BlockSpec((tm, KV_W), lambda i: (i, k_off // KV_W)),
                  pl.BlockSpec((tm, KV_W), lambda i: (i, v_off // KV_W)),
                  pl.BlockSpec((tm, HEAD_DIM), lambda i: (i, 0)),
                  pl.BlockSpec((tm, HEAD_DIM), lambda i: (i, 0)),
                  pl.BlockSpec((1, HEAD_DIM), lambda i: (0, 0)),
                  pl.BlockSpec((1, HEAD_DIM), lambda i: (0, 0))],
        out_specs=[pl.BlockSpec((tm, MIX_W), lambda i: (i, 0)),
                   pl.BlockSpec((tm, KV_W), lambda i: (i, 0)),
                   pl.BlockSpec((tm, N_KV_HEADS * V_AUG), lambda i: (i, 0))],
        out_shape=[jax.ShapeDtypeStruct((t, MIX_W), BF16),
                   jax.ShapeDtypeStruct((t, KV_W), BF16),
                   jax.ShapeDtypeStruct((t, N_KV_HEADS * V_AUG), BF16)],
        compiler_params=_cparams("parallel"), name="qk_prep",
    )(p, p, p, cos, sin, qg, kg)


def _flash_kernel(sink_ref, q_ref, k_ref, v_ref, o_ref, qs_ref, m_ref, acc_ref, *, tq, use_sink):
    g, ki = pl.program_id(0), pl.program_id(2)

    @pl.when(ki == 0)
    def _():
        for h in range(GQA_GROUP):
            qs_ref[h * tq:(h + 1) * tq, :] = q_ref[:, h * HEAD_DIM:(h + 1) * HEAD_DIM]
            if use_sink:
                m_ref[h * tq:(h + 1) * tq, :] = jnp.full((tq, 1), sink_ref[g * GQA_GROUP + h] * LOG2E, F32)
        acc_ref[:, :HEAD_DIM] = jnp.zeros((GQA_GROUP * tq, HEAD_DIM), F32)
        if use_sink:
            acc_ref[:, HEAD_DIM:] = jnp.ones((GQA_GROUP * tq, HEAD_DIM), F32)
        else:
            m_ref[...] = jnp.full_like(m_ref, -jnp.inf)
            acc_ref[:, HEAD_DIM:] = jnp.zeros((GQA_GROUP * tq, HEAD_DIM), F32)

    k, v = k_ref[...], v_ref[...]

    def scores(h):
        return lax.dot_general(qs_ref[h * tq:(h + 1) * tq, :], k, _NT, preferred_element_type=F32)

    def weigh(h, s):
        rows = slice(h * tq, (h + 1) * tq)
        m_prev = m_ref[rows, :]
        m_new = jnp.maximum(m_prev, jnp.max(s, axis=-1, keepdims=True))
        m_ref[rows, :] = m_new
        p = jnp.exp2(s - m_new).astype(BF16)
        return jnp.exp2(m_prev - m_new), jnp.dot(p, v, preferred_element_type=F32)

    s_next = scores(0)
    upd = []
    for h in range(GQA_GROUP):
        s_cur = s_next
        if h + 1 < GQA_GROUP:
            s_next = scores(h + 1)
        upd.append(weigh(h, s_cur))
    for h, (alpha, pv) in enumerate(upd):
        rows = slice(h * tq, (h + 1) * tq)
        acc_ref[rows, :] = alpha * acc_ref[rows, :] + pv

    @pl.when(ki == pl.num_programs(2) - 1)
    def _():
        out = acc_ref[:, :HEAD_DIM] / acc_ref[:, HEAD_DIM:]
        for h in range(GQA_GROUP):
            o_ref[:, h * HEAD_DIM:(h + 1) * HEAD_DIM] = out[h * tq:(h + 1) * tq, :].astype(BF16)


def _pick_tk(tk_total):
    for c in (1280, 1024, 512, 256):
        if tk_total % c == 0:
            return c
    return 128


def _dense_gqa(q, k, v, sink=None):
    tq_total, tk_total = q.shape[0], k.shape[0]
    tq = 512 if tq_total % 512 == 0 else 256
    tk = _pick_tk(tk_total)
    use_sink = sink is not None
    sink_arr = sink if use_sink else jnp.zeros((N_Q_HEADS,), F32)
    gw = GQA_GROUP * HEAD_DIM
    return pl.pallas_call(
        functools.partial(_flash_kernel, tq=tq, use_sink=use_sink),
        grid=(N_KV_HEADS, tq_total // tq, tk_total // tk),
        in_specs=[pl.BlockSpec(memory_space=pltpu.SMEM),
                  pl.BlockSpec((tq, gw), lambda g, qi, ki: (qi, g)),
                  pl.BlockSpec((tk, HEAD_DIM), lambda g, qi, ki: (ki, g)),
                  pl.BlockSpec((tk, V_AUG), lambda g, qi, ki: (ki, g))],
        out_specs=pl.BlockSpec((tq, gw), lambda g, qi, ki: (qi, g)),
        out_shape=jax.ShapeDtypeStruct((tq_total, MIX_W), BF16),
        scratch_shapes=[pltpu.VMEM((GQA_GROUP * tq, HEAD_DIM), BF16),
                        pltpu.VMEM((GQA_GROUP * tq, 1), F32),
                        pltpu.VMEM((GQA_GROUP * tq, V_AUG), F32)],
        compiler_params=_cparams("parallel", "parallel", "arbitrary"), name="dense_gqa",
    )(sink_arr, q, k, v)


def _window_kernel(sink_ref, q_ref, kp_ref, k0_ref, kx_ref, kc_ref, vp_ref, v0_ref, vx_ref, vc_ref, o_ref):
    g, n = pl.program_id(0), pl.program_id(1)
    nb = pl.num_programs(1)
    tc = kc_ref.shape[0]
    qs = jnp.concatenate([q_ref[:, h * HEAD_DIM:(h + 1) * HEAD_DIM] for h in range(GQA_GROUP)], axis=0)
    kcat = jnp.concatenate([kp_ref[...], k0_ref[...], kx_ref[...], kc_ref[...]], axis=0)
    vcat = jnp.concatenate([vp_ref[...], v0_ref[...], vx_ref[...], vc_ref[...]], axis=0)
    s = lax.dot_general(qs, kcat, (((1,), (1,)), ((), ())), preferred_element_type=F32)
    row = lax.broadcasted_iota(jnp.int32, s.shape, 0) % BLK
    col = lax.broadcasted_iota(jnp.int32, s.shape, 1)
    rel = col - BLK - row
    valid = (jnp.abs(rel) <= BLK) & ((col >= BLK) | (n > 0)) & ((col < 2 * BLK) | (n < nb - 1))
    valid = valid | (col >= 3 * BLK)
    s = jnp.where(valid, s, -jnp.inf)
    sink = jnp.concatenate([jnp.full((BLK, 1), sink_ref[g * GQA_GROUP + h] * LOG2E, F32)
                            for h in range(GQA_GROUP)], axis=0)
    m = jnp.maximum(jnp.max(s, axis=-1, keepdims=True), sink)
    p = jnp.exp2(s - m).astype(BF16)
    pv = jnp.dot(p, vcat, preferred_element_type=F32)
    out = pv[:, :HEAD_DIM] / (pv[:, HEAD_DIM:] + jnp.exp2(sink - m))
    for h in range(GQA_GROUP):
        o_ref[:, h * HEAD_DIM:(h + 1) * HEAD_DIM] = out[h * BLK:(h + 1) * BLK, :].astype(BF16)


def _window_gqa(q, k, v, kc, vc, sink):
    t = q.shape[0]
    nb = t // BLK
    tc = kc.shape[0]
    gw = GQA_GROUP * HEAD_DIM
    blk = lambda f: pl.BlockSpec((BLK, HEAD_DIM), f)
    vblk = lambda f: pl.BlockSpec((BLK, V_AUG), f)
    prev = lambda g, n: (jnp.maximum(n - 1, 0), g)
    cur = lambda g, n: (n, g)
    nxt = lambda g, n: (jnp.minimum(n + 1, nb - 1), g)
    ctx = pl.BlockSpec((tc, HEAD_DIM), lambda g, n: (0, g))
    vctx = pl.BlockSpec((tc, V_AUG), lambda g, n: (0, g))
    return pl.pallas_call(
        _window_kernel, grid=(N_KV_HEADS, nb),
        in_specs=[pl.BlockSpec(memory_space=pltpu.SMEM),
                  pl.BlockSpec((BLK, gw), lambda g, n: (n, g)),
                  blk(prev), blk(cur), blk(nxt), ctx, vblk(prev), vblk(cur), vblk(nxt), vctx],
        out_specs=pl.BlockSpec((BLK, gw), lambda g, n: (n, g)),
        out_shape=jax.ShapeDtypeStruct((t, MIX_W), BF16),
        compiler_params=_cparams("parallel", "parallel"), name="window_gqa",
    )(sink, q, k, k, k, kc, v, v, v, vc)


def _sgu_kernel(u_ref, z_ref, ng_ref, ws_ref, bs_ref, o_ref, *, chunks):
    for c in range(chunks):
        rows = slice(c * BLK, (c + 1) * BLK)
        for g in range(SGU_GROUPS):
            cols = slice(g * 128, (g + 1) * 128)
            z = _gelu_tanh(z_ref[rows, cols])
            mu = jnp.mean(z, axis=-1, keepdims=True)
            zc = z - mu
            var = jnp.mean(zc * zc, axis=-1, keepdims=True)
            vn = zc * lax.rsqrt(var + EPS) * ng_ref[:, cols]
            vm = jnp.dot(ws_ref[g], vn.astype(BF16), preferred_element_type=F32) + bs_ref[g]
            o_ref[rows, cols] = (_gelu_tanh(u_ref[rows, cols]) * vm).astype(BF16)


def _sgu(p, norm_g, w_s, b_s):
    t = p.shape[0]
    chunks = 2
    tm = chunks * BLK
    bs = jnp.broadcast_to(b_s[:, :, None], (SGU_GROUPS, BLK, 128))
    return pl.pallas_call(
        functools.partial(_sgu_kernel, chunks=chunks), grid=(t // tm,),
        in_specs=[pl.BlockSpec((tm, MIX_W), lambda i: (i, EV_U // MIX_W)),
                  pl.BlockSpec((tm, MIX_W), lambda i: (i, EV_Z // MIX_W)),
                  pl.BlockSpec((1, MIX_W), lambda i: (0, 0)),
                  pl.BlockSpec((SGU_GROUPS, BLK, BLK), lambda i: (0, 0, 0)),
                  pl.BlockSpec((SGU_GROUPS, BLK, 128), lambda i: (0, 0, 0))],
        out_specs=pl.BlockSpec((tm, MIX_W), lambda i: (i, 0)),
        out_shape=jax.ShapeDtypeStruct((t, MIX_W), BF16),
        compiler_params=_cparams("parallel"), name="chunk_sgu",
    )(p, p, norm_g.reshape(1, MIX_W), w_s.astype(BF16), bs)


def _seg_sum(x, e):
    parts = [jnp.dot(x[:, b * 128:(b + 1) * 128], e, precision=HI, preferred_element_type=F32)
             for b in range(x.shape[1] // 128)]
    return jnp.concatenate(parts, axis=1)


def _feat_kernel(r_ref, k_ref, v_ref, gl_ref, wl_ref, al_ref, g2_ref, w2_ref, a2_ref, w0_ref, a0_ref,
                 kk_ref, ka_ref, rk_ref, e_ref, tri_ref, ld_ref, cum_ref, kd_ref, ad_ref, kko_ref, bon_ref, go_ref):
    e = e_ref[...]
    r, k, v = r_ref[...], k_ref[...], v_ref[...]
    go_ref[...] = jnp.dot(_sigmoid(gl_ref[...]), g2_ref[...], preferred_element_type=F32)
    lw = w0_ref[...] + jnp.dot(jnp.tanh(wl_ref[...]), w2_ref[...], preferred_element_type=F32)
    nl = -lw
    softplus = jnp.maximum(nl, 0.0) + jnp.log(1.0 + jnp.exp(-jnp.abs(nl)))
    ld = -jnp.exp(-softplus - 0.5)
    ld_ref[...] = ld
    cum_ref[:, :MIX_W] = jnp.dot(tri_ref[0], ld[:, :MIX_W], precision=HI, preferred_element_type=F32)
    cum_ref[:, MIX_W:] = jnp.dot(tri_ref[1], ld[:, MIX_W:], precision=HI, preferred_element_type=F32)
    a = _sigmoid(a0_ref[...] + jnp.dot(al_ref[...], a2_ref[...], preferred_element_type=F32))
    ad_ref[...] = a
    kk = k * kk_ref[...]
    nrm = jnp.maximum(jnp.sqrt(_seg_sum(kk * kk, e)), 1e-12)
    kko_ref[...] = kk / nrm
    ka = ka_ref[...]
    kd0 = k * (1.0 + (a[:, :MIX_W] - 1.0) * ka)
    kd1 = k * (1.0 + (a[:, MIX_W:] - 1.0) * ka)
    kd_ref[:, :MIX_W] = kd0
    kd_ref[:, MIX_W:] = kd1
    bon_ref[...] = _seg_sum(r * (kd0 + kd1) * rk_ref[...], e) * v


def _rwkv_features(zs, g2p, w2b, a2b, w0, a0, k_k, k_a, r_k, e):
    t = zs.shape[0]
    tm = 256
    col = lambda w, off: pl.BlockSpec((tm, w), lambda i: (i, off // w))
    full = lambda a: pl.BlockSpec(a.shape, lambda i: (0,) * a.ndim)
    wide = lambda w: pl.BlockSpec((tm, w), lambda i: (i, 0))
    row = jnp.arange(tm)
    same_chunk = (row[:, None] // SCAN_CHUNK) == (row[None, :] // SCAN_CHUNK)
    tri = jnp.stack([same_chunk & (row[None, :] <= row[:, None]),
                     same_chunk & (row[None, :] >= row[:, None])]).astype(F32)
    params = [g2p, w2b, a2b, w0, a0, k_k, k_a, r_k, e, tri]
    return pl.pallas_call(
        _feat_kernel, grid=(t // tm,),
        in_specs=[col(MIX_W, OD_R), col(MIX_W, OD_K), col(MIX_W, OD_V), col(D_GATE_PAD, OD_G),
                  col(128, OD_W), col(128, OD_A)] + [full(a) for a in params],
        out_specs=[wide(2 * MIX_W)] * 4 + [wide(MIX_W)] * 3,
        out_shape=[jax.ShapeDtypeStruct((t, 2 * MIX_W), F32)] * 4 + [jax.ShapeDtypeStruct((t, MIX_W), F32)] * 3,
        compiler_params=_cparams("parallel"), name="rwkv_features",
    )(zs, zs, zs, zs, zs, zs, *params)


def _bdot(a, b, dims=(((1,), (0,)), ((), ()))):
    return lax.dot_general(a.astype(BF16), b.astype(BF16), dims, preferred_element_type=F32)


def _scan_masks():
    L = SCAN_CHUNK
    r = jnp.arange(2 * L)[:, None]
    c = jnp.arange(2 * L)[None, :]
    same = (r // L) == (c // L)
    out = []
    for reverse in (False, True):
        before = (c > r) if reverse else (c < r)
        out += [same & before, same & (before | (c == r))]
        b = 1
        while b < L:
            late_r, late_c = (r // b) % 2 == 1, (c // b) % 2 == 1
            couple = (~late_r & late_c) if reverse else (late_r & ~late_c)
            out.append(((r // (2 * b)) == (c // (2 * b))) & couple)
            b *= 2
    return jnp.stack(out).astype(F32), same.astype(F32)


def _chunk_group(streams, mask_ref, nlv, same, eye, m0, m1):
    L = SCAN_CHUNK
    n = len(streams)
    pre = []
    for ld, cum, kd, ad, r, v, kk, s_prev, d in streams:
        g_inc = jnp.exp(cum)
        g_exc = jnp.exp(cum - ld)
        g_inv = jnp.exp(-cum)
        last = 0 if d else L - 1
        g_tot = g_inc[last:last + 1, :]
        at = -kk * g_exc
        rt = r * g_inc
        bt = (kk * ad * g_inv).astype(BF16)
        kt = (kd * g_inv).astype(BF16)
        lar = jnp.concatenate([at * m0, at * m1, rt * m0, rt * m1], axis=0).astype(BF16)
        vbd = jnp.concatenate([v * m0, v * m1], axis=0).astype(BF16)
        pre.append((lar, bt, kt, vbd, v.astype(BF16), g_tot, s_prev, d * nlv))
    gram = [_bdot(lar, jnp.concatenate([bt, bt, kt, kt], axis=0), _NT) for lar, bt, kt, _, _, _, _, _ in pre]
    xs = [_bdot(p[0], p[6], _NT) for p in pre]
    n_ab, m_ak, m_r = [], [], []
    for g, p in zip(gram, pre):
        strict, incl = mask_ref[p[7]] > 0, mask_ref[p[7] + 1] > 0
        n_ab.append(jnp.where(strict, g[:2 * L, :2 * L], 0.0))
        m_ak.append(jnp.where(strict, g[:2 * L, 2 * L:], 0.0).astype(BF16))
        m_r.append(jnp.concatenate([jnp.where(incl, g[2 * L:, :2 * L], 0.0),
                                    jnp.where(incl, g[2 * L:, 2 * L:], 0.0)], axis=1).astype(BF16))
    x = [xs[i][:2 * L] + _bdot(m_ak[i], pre[i][3]) for i in range(n)]
    tinv = [eye + n_ab[i] * mask_ref[pre[i][7] + 2] for i in range(n)]
    for q in range(3, nlv):
        tb = [t.astype(BF16) for t in tinv]
        half = [_bdot(tb[i], n_ab[i] * mask_ref[pre[i][7] + q]) for i in range(n)]
        tinv = [tinv[i] + _bdot(half[i], tb[i]) for i in range(n)]
    ubd = [_bdot(tinv[i], x[i]) for i in range(n)]
    ybd = [xs[i][2 * L:] + _bdot(m_r[i], jnp.concatenate([ubd[i].astype(BF16), pre[i][3]], axis=0)) for i in range(n)]
    out = []
    for i in range(n):
        _, bt, kt, _, vb, g_tot, s_prev, _ = pre[i]
        u = (ubd[i][:L, :] + ubd[i][L:, :]).astype(BF16)
        upd = _bdot(jnp.concatenate([u, vb], axis=0), jnp.concatenate([bt, kt], axis=0), _TN)
        out.append((ybd[i][:L, :] + ybd[i][L:, :], (s_prev + upd * same) * g_tot))
    return out


def _scan_kernel(ldf_ref, cumf_ref, kdf_ref, adf_ref, rf_ref, vf_ref, kkf_ref,
                 ldb_ref, cumb_ref, kdb_ref, adb_ref, rb_ref, vb_ref, kkb_ref,
                 s0f_ref, s0b_ref, mask_ref, same_ref, eye_ref,
                 yf_ref, yb_ref, sff_ref, sfb_ref, sf_ref, sb_ref):
    L = SCAN_CHUNK
    c = pl.program_id(0)

    @pl.when(c == 0)
    def _():
        sf_ref[...] = s0f_ref[...]
        sb_ref[...] = s0b_ref[...]

    lane = lax.broadcasted_iota(jnp.int32, (L, 2 * RWKV_HEAD), 1)
    m0 = (lane < RWKV_HEAD).astype(F32)
    m1 = 1.0 - m0
    nlv = mask_ref.shape[0] // 2
    same, eye = same_ref[...], eye_ref[...]
    dirs = ((ldf_ref, cumf_ref, kdf_ref, adf_ref, rf_ref, vf_ref, kkf_ref, sf_ref, yf_ref, 0),
            (ldb_ref, cumb_ref, kdb_ref, adb_ref, rb_ref, vb_ref, kkb_ref, sb_ref, yb_ref, 1))
    for p0 in range(0, RWKV_PAIRS, SCAN_GROUP_PAIRS):
        streams, dests = [], []
        for p in range(p0, p0 + SCAN_GROUP_PAIRS):
            cols = slice(p * 128, (p + 1) * 128)
            for ld_ref, cum_ref, kd_ref, ad_ref, r_ref, v_ref, kk_ref, s_ref, y_ref, d in dirs:
                streams.append((ld_ref[:, cols], cum_ref[:, cols], kd_ref[:, cols], ad_ref[:, cols],
                                r_ref[:, cols], v_ref[:, cols], kk_ref[:, cols], s_ref[p], d))
                dests.append((y_ref, s_ref, p, cols))
        for (y_ref, s_ref, p, cols), (y, s_new) in zip(dests, _chunk_group(streams, mask_ref, nlv, same, eye, m0, m1)):
            y_ref[:, cols] = y
            s_ref[p] = s_new

    @pl.when(c == pl.num_programs(0) - 1)
    def _():
        sff_ref[...] = sf_ref[...]
        sfb_ref[...] = sb_ref[...]


def _rwkv_scan(ld, cum, kd, ad, zs, kk, s0f, s0b):
    t = zs.shape[0]
    L = SCAN_CHUNK
    nc = t // L
    masks, same = _scan_masks()
    eye = jnp.eye(2 * L, dtype=F32)
    specs = []
    for d, cidx in ((0, lambda c: c), (1, lambda c: nc - 1 - c)):
        dirblk = pl.BlockSpec((L, MIX_W), lambda c, cidx=cidx, d=d: (cidx(c), d))
        specs += [dirblk, dirblk, dirblk, dirblk,
                  pl.BlockSpec((L, MIX_W), lambda c, cidx=cidx: (cidx(c), OD_R // MIX_W)),
                  pl.BlockSpec((L, MIX_W), lambda c, cidx=cidx: (cidx(c), OD_V // MIX_W)),
                  pl.BlockSpec((L, MIX_W), lambda c, cidx=cidx: (cidx(c), 0))]
    sblk = pl.BlockSpec((RWKV_PAIRS, 128, 128), lambda c: (0, 0, 0))
    const = lambda a: pl.BlockSpec(a.shape, lambda c: (0,) * a.ndim)
    feats = (ld, cum, kd, ad, zs, zs, kk)
    return pl.pallas_call(
        _scan_kernel, grid=(nc,),
        in_specs=specs + [sblk, sblk, const(masks), const(same), const(eye)],
        out_specs=[pl.BlockSpec((L, MIX_W), lambda c: (c, 0)),
                   pl.BlockSpec((L, MIX_W), lambda c: (nc - 1 - c, 0)), sblk, sblk],
        out_shape=[jax.ShapeDtypeStruct((t, MIX_W), F32)] * 2 + [jax.ShapeDtypeStruct((RWKV_PAIRS, 128, 128), F32)] * 2,
        scratch_shapes=[pltpu.VMEM((RWKV_PAIRS, 128, 128), F32)] * 2,
        compiler_params=_cparams("arbitrary"), name="rwkv_scan",
    )(*feats, *feats, s0f, s0b, masks, same, eye)


def _readout_kernel(yf_ref, yb_ref, bon_ref, g_ref, lw_ref, lb_ref, e_ref, o_ref):
    e = e_ref[...]
    y = yf_ref[...] + yb_ref[...]
    mu = _seg_sum(y, e) * (1.0 / RWKV_HEAD)
    yc = y - mu
    var = _seg_sum(yc * yc, e) * (1.0 / RWKV_HEAD)
    yn = yc * lax.rsqrt(var + GN_EPS) * lw_ref[...] + lb_ref[...]
    o_ref[...] = ((yn + bon_ref[...]) * g_ref[...]).astype(BF16)


def _rwkv_readout(yf, yb, bonus, g, ln_w, ln_b, e):
    t = yf.shape[0]
    tm = 256
    wide = pl.BlockSpec((tm, MIX_W), lambda i: (i, 0))
    vec = pl.BlockSpec((1, MIX_W), lambda i: (0, 0))
    return pl.pallas_call(
        _readout_kernel, grid=(t // tm,),
        in_specs=[wide, wide, wide, wide, vec, vec, pl.BlockSpec((128, 128), lambda i: (0, 0))],
        out_specs=wide, out_shape=jax.ShapeDtypeStruct((t, MIX_W), BF16),
        compiler_params=_cparams("parallel"), name="rwkv_readout",
    )(yf, yb, bonus, g, ln_w, ln_b, e)


def _final_norm_kernel(x_ref, g_ref, o_ref):
    xv = x_ref[...]
    o_ref[...] = xv * lax.rsqrt(jnp.mean(xv * xv, axis=-1, keepdims=True) + EPS) * g_ref[...]


def _final_norm(x, g):
    t, d = x.shape
    tm = 256
    return pl.pallas_call(
        _final_norm_kernel, grid=(t // tm,),
        in_specs=[pl.BlockSpec((tm, d), lambda i: (i, 0)), pl.BlockSpec((1, d), lambda i: (0, 0))],
        out_specs=pl.BlockSpec((tm, d), lambda i: (i, 0)),
        out_shape=jax.ShapeDtypeStruct((t, d), F32),
        compiler_params=_cparams("parallel"), name="final_norm",
    )(x, g)


def _rope_tables(rows):
    half = HEAD_DIM // 2
    inv = ROPE_THETA ** (-jnp.arange(0, half, 2, dtype=F32) / half)
    row = jnp.repeat(jnp.arange(rows, dtype=F32), GRID_W)
    col = jnp.tile(jnp.arange(GRID_W, dtype=F32), rows)
    ang_r = row[:, None] * inv[None, :]
    ang_c = col[:, None] * inv[None, :]
    ang = jnp.concatenate([ang_r, ang_r, ang_c, ang_c], axis=-1)
    return jnp.cos(ang), jnp.sin(ang)


def _even_layer(x, ctx, mx, mc, n1g, w_in, w_out, qg, kg, vng, w_s, b_s, tabs, with_ctx):
    o_k = N_Q_HEADS * HEAD_DIM
    o_v = o_k + KV_W
    o_u = o_v + KV_W
    o_z = o_u + MIX_W
    wp = jnp.concatenate([w_in[:, :o_k], w_in[:, o_u:o_z], w_in[:, o_z:], w_in[:, o_k:o_v], w_in[:, o_v:o_u]],
                         axis=1).astype(BF16)
    wa, wb = w_out[:MIX_W].astype(BF16), w_out[MIX_W:].astype(BF16)
    qg, kg = qg.reshape(1, HEAD_DIM), kg.reshape(1, HEAD_DIM)
    (cos_x, sin_x), (cos_c, sin_c) = tabs
    px = _proj(x, n1g, mx[0], mx[1], wp)
    pc = _proj(ctx, n1g, mc[0], mc[1], wp)
    qx, kx, vx = _qk_prep(px, cos_x, sin_x, qg, kg, EV_Q, EV_K, EV_V, True)
    qc, kc, vc = _qk_prep(pc, cos_c, sin_c, qg, kg, EV_Q, EV_K, EV_V, True)
    ax = _dense_gqa(qx, jnp.concatenate([kx, kc], axis=0), jnp.concatenate([vx, vc], axis=0))
    bx = _sgu(px, vng, w_s, b_s)
    x = _outproj(ax, bx, x, mx[2], wa, wb)
    if with_ctx:
        ac = _dense_gqa(qc, kc, vc)
        bc = _sgu(pc, vng, w_s, b_s)
        ctx = _outproj(ac, bc, ctx, mc[2], wa, wb)
    return x, ctx


def _odd_layer(x, ctx, mx, mc, n1g, w_in, w_out, sink, shift_mu, w0, w2, a0, a2, g2, k_k, k_a, r_k, ln_w, ln_b,
               tabs, with_ctx):
    o_k = N_Q_HEADS * HEAD_DIM
    o_v = o_k + KV_W
    c_in = o_v + KV_W
    o_g = 3 * MIX_W
    o_w = o_g + D_GATE_LORA
    o_a = o_w + 2 * D_LORA
    d = w_in.shape[0]
    wr = w_in[:, c_in:]
    gpad = jnp.zeros((d, D_GATE_PAD - D_GATE_LORA), w_in.dtype)
    wp = jnp.concatenate([w_in[:, :o_k], wr[:, :o_g], w_in[:, o_k:o_v], w_in[:, o_v:c_in],
                          wr[:, o_g:o_w], gpad, wr[:, o_w:o_a], wr[:, o_a:]], axis=1).astype(BF16)
    taps = jnp.stack([shift_mu[0], 1.0 - shift_mu[0] - shift_mu[1], shift_mu[1]], axis=0)
    ident = jnp.tile(jnp.array([[0.0], [1.0], [0.0]], F32), (1, 1))
    cw = jnp.concatenate([jnp.tile(ident, (1, MIX_W)), taps[:, :o_g], jnp.tile(ident, (1, 2 * KV_W)),
                          taps[:, o_g:o_w], jnp.tile(ident, (1, D_GATE_PAD - D_GATE_LORA)),
                          taps[:, o_w:o_a], taps[:, o_a:]], axis=1)
    wa, wb = w_out[:MIX_W].astype(BF16), w_out[MIX_W:].astype(BF16)
    g2p = jnp.concatenate([g2, jnp.zeros((D_GATE_PAD - D_GATE_LORA, MIX_W), g2.dtype)], axis=0)
    zl = jnp.zeros((D_LORA, MIX_W), F32)
    w2b = jnp.concatenate([jnp.concatenate([w2[0], zl], axis=1), jnp.concatenate([zl, w2[1]], axis=1)], axis=0)
    a2b = jnp.concatenate([jnp.concatenate([a2[0], zl], axis=1), jnp.concatenate([zl, a2[1]], axis=1)], axis=0)
    w0f, a0f = w0.reshape(1, 2 * MIX_W), a0.reshape(1, 2 * MIX_W)
    kkv, kav, rkv = k_k.reshape(1, MIX_W), k_a.reshape(1, MIX_W), r_k.reshape(1, MIX_W)
    lnw, lnb = ln_w.reshape(1, MIX_W), ln_b.reshape(1, MIX_W)
    lane = jnp.arange(128)
    e = (lane[:, None] // RWKV_HEAD == lane[None, :] // RWKV_HEAD).astype(F32)
    ones_g = jnp.ones((1, HEAD_DIM), F32)
    (cos_x, sin_x), (cos_c, sin_c) = tabs

    px = _proj(x, n1g, mx[0], mx[1], wp, cw)
    pc = _proj(ctx, n1g, mc[0], mc[1], wp, cw)
    qx, kx, vx = _qk_prep(px, cos_x, sin_x, ones_g, ones_g, OD_Q, OD_AK, OD_AV, False)
    qc, kc, vc = _qk_prep(pc, cos_c, sin_c, ones_g, ones_g, OD_Q, OD_AK, OD_AV, False)
    cx = _window_gqa(qx, kx, vx, kc, vc, sink)

    fparams = (g2p, w2b, a2b, w0f, a0f, kkv, kav, rkv, e)
    ld_x, cum_x, kd_x, ad_x, kk_x, bon_x, g_x = _rwkv_features(px, *fparams)
    ld_c, cum_c, kd_c, ad_c, kk_c, bon_c, g_c = _rwkv_features(pc, *fparams)
    s0 = jnp.zeros((RWKV_PAIRS, 128, 128), F32)
    y_cf, y_cb, s_cf, s_cb = _rwkv_scan(ld_c, cum_c, kd_c, ad_c, pc, kk_c, s0, s0)
    y_xf, y_xb, _, _ = _rwkv_scan(ld_x, cum_x, kd_x, ad_x, px, kk_x, s_cf, s_cb)
    dx = _rwkv_readout(y_xf, y_xb, bon_x, g_x, lnw, lnb, e)
    x = _outproj(cx, dx, x, mx[2], wa, wb)
    if with_ctx:
        cc = _dense_gqa(qc, kc, vc, sink)
        dc = _rwkv_readout(y_cf, y_cb, bon_c, g_c, lnw, lnb, e)
        ctx = _outproj(cc, dc, ctx, mc[2], wa, wb)
    return x, ctx


def kernel(x, c, ctx, c_ctx, ada_w, ada_b, norm1_g, norm2_g, ffn_w_in, ffn_conv_w, ffn_conv_b, ffn_w_out,
           ev_w_in, ev_w_out, a_q_norm_g, a_k_norm_g, b_v_norm_g, b_spatial_w, b_spatial_b,
           od_w_in, od_w_out, c_sink, d_shift_mu, d_w0, d_w2, d_a0, d_a2, d_g2, d_k_k, d_k_a, d_r_k,
           d_ln_w, d_ln_b, final_norm_g):
    bsz, t, d = x.shape
    assert bsz == 1 and d == D_MODEL
    tc = ctx.shape[1]
    depth = ada_w.shape[0]
    xs, cs = x[0], ctx[0]
    tabs = (_rope_tables(t // GRID_W), (jnp.ones((tc, HEAD_DIM), F32), jnp.zeros((tc, HEAD_DIM), F32)))
    sv = jnp.stack([jax.nn.silu(c[0]), jax.nn.silu(c_ctx)], axis=1)
    mods = _ada_mod(sv, ada_w, ada_b)
    for l in range(depth):
        with_ctx = l < depth - 1
        mx = [mods[l, 0, k * d:(k + 1) * d].reshape(1, d) for k in range(6)]
        mc = [mods[l, 1, k * d:(k + 1) * d].reshape(1, d) for k in range(6)]
        n1g, n2g = norm1_g[l].reshape(1, d), norm2_g[l].reshape(1, d)
        i = l // 2
        if l % 2 == 0:
            xs, cs = _even_layer(xs, cs, mx, mc, n1g, ev_w_in[i], ev_w_out[i], a_q_norm_g[i], a_k_norm_g[i],
                                 b_v_norm_g[i], b_spatial_w[i], b_spatial_b[i], tabs, with_ctx)
        else:
            xs, cs = _odd_layer(xs, cs, mx, mc, n1g, od_w_in[i], od_w_out[i], c_sink[i], d_shift_mu[i], d_w0[i],
                                d_w2[i], d_a0[i], d_a2[i], d_g2[i], d_k_k[i], d_k_a[i], d_r_k[i], d_ln_w[i],
                                d_ln_b[i], tabs, with_ctx)
        fw_in, fw_out = ffn_w_in[l].astype(BF16), ffn_w_out[l].astype(BF16)
        fcb = ffn_conv_b[l].reshape(1, 2 * D_FF)
        xs = _ffn(xs, n2g, mx[3], mx[4], mx[5], fw_in, ffn_conv_w[l], fcb, fw_out)
        if with_ctx:
            cs = _ffn(cs, n2g, mc[3], mc[4], mc[5], fw_in, ffn_conv_w[l], fcb, fw_out)
    return _final_norm(xs, final_norm_g.reshape(1, d))[None]
```

```python
import functools
import math

import jax
import jax.numpy as jnp
from jax import lax
from jax.experimental import pallas as pl
from jax.experimental.pallas import tpu as pltpu

F32 = jnp.float32
BF16 = jnp.bfloat16

D_MODEL = 2048
GRID_W = 64
HEAD_DIM = 128
BLK = 128
MIX_W = D_MODEL // 2
N_Q_HEADS = MIX_W // HEAD_DIM
N_KV_HEADS = N_Q_HEADS // 4
GQA_GROUP = N_Q_HEADS // N_KV_HEADS
KV_W = N_KV_HEADS * HEAD_DIM
SGU_GROUPS = MIX_W // 128
RWKV_HEAD = 64
RWKV_PAIRS = MIX_W // (2 * RWKV_HEAD)
D_GATE_LORA = 160
D_GATE_PAD = 256
D_LORA = 64
D_FF = 5632
ROPE_THETA = 10000.0
EPS = 1e-6
GN_EPS = 64e-5
LOG2E = math.log2(math.e)
Q_SCALE = HEAD_DIM ** -0.5 * LOG2E
V_AUG = 2 * HEAD_DIM

HALO = 16
SCAN_CHUNK = 64
SCAN_GROUP_PAIRS = 8
VMEM_LIMIT = 56 * 1024 * 1024

EV_Q, EV_U, EV_Z, EV_K, EV_V = 0, 1024, 2048, 3072, 3328
EV_N = 3584
OD_Q, OD_R, OD_K, OD_V, OD_AK, OD_AV, OD_G, OD_W, OD_A = 0, 1024, 2048, 3072, 4096, 4352, 4608, 4864, 4992
OD_N = 5120


_NT = (((1,), (1,)), ((), ()))
_TN = (((0,), (0,)), ((), ()))


def _cparams(*sem):
    return pltpu.CompilerParams(dimension_semantics=sem, vmem_limit_bytes=VMEM_LIMIT)


def _sigmoid(x):
    return 1.0 / (1.0 + jnp.exp(-x))


def _gelu_tanh(x):
    return 0.5 * x * (1.0 + jnp.tanh(math.sqrt(2.0 / math.pi) * (x + 0.044715 * (x * x * x))))


def _norm_mod(xv, g, sh, sc):
    ms = jnp.mean(xv * xv, axis=-1, keepdims=True)
    y = xv * lax.rsqrt(ms + EPS) * g
    return y * (1.0 + sc) + sh


def _ada_kernel(sv_ref, w_ref, b_ref, o_ref):
    kc = 256
    nk = w_ref.shape[1] // kc
    tn = w_ref.shape[2]

    def body(c, acc):
        a0, a1 = acc
        k0 = pl.multiple_of(c * kc, kc)
        w = w_ref[0, pl.ds(k0, kc), :]
        s = sv_ref[pl.ds(k0, kc), :]
        a0 = a0 + jnp.sum(w * s[:, 0:1], axis=0, keepdims=True)
        a1 = a1 + jnp.sum(w * s[:, 1:2], axis=0, keepdims=True)
        return a0, a1

    z = jnp.zeros((1, tn), F32)
    a0, a1 = lax.fori_loop(0, nk, body, (z, z))
    b = b_ref[0]
    o_ref[0] = jnp.concatenate([a0 + b, a1 + b], axis=0)


def _ada_mod(sv, ada_w, ada_b):
    depth, d, n = ada_w.shape
    tn = 1024
    return pl.pallas_call(
        _ada_kernel,
        grid=(depth, n // tn),
        in_specs=[pl.BlockSpec((d, 2), lambda l, j: (0, 0)),
                  pl.BlockSpec((1, d, tn), lambda l, j: (l, 0, j)),
                  pl.BlockSpec((1, 1, tn), lambda l, j: (l, 0, j))],
        out_specs=pl.BlockSpec((1, 2, tn), lambda l, j: (l, 0, j)),
        out_shape=jax.ShapeDtypeStruct((depth, 2, n), F32),
        compiler_params=_cparams("parallel", "parallel"),
        name="ada_mod",
    )(sv, ada_w, ada_b.reshape(depth, 1, n))


def _fill_hn(hn_ref, x_ref, xp_ref, xn_ref, g_ref, sh_ref, sc_ref, tm, first, last):
    g, sh, sc = g_ref[...], sh_ref[...], sc_ref[...]
    rc = 128
    for r in range(tm // rc):
        hn_ref[HALO + r * rc:HALO + (r + 1) * rc, :] = _norm_mod(x_ref[r * rc:(r + 1) * rc, :], g, sh, sc).astype(BF16)
    hp = _norm_mod(xp_ref[...], g, sh, sc)
    hn_ref[0:HALO, :] = jnp.where(first, 0.0, hp).astype(BF16)
    hx = _norm_mod(xn_ref[...], g, sh, sc)
    hn_ref[HALO + tm:2 * HALO + tm, :] = jnp.where(last, 0.0, hx).astype(BF16)


def _conv3(z, cw, tm):
    rows = z.shape[0]
    zp = pltpu.roll(z, 1, 0)
    zn = pltpu.roll(z, rows - 1, 0)
    out = cw[0:1, :] * zp + cw[1:2, :] * z + cw[2:3, :] * zn
    return out[HALO:HALO + tm, :]


def _proj_kernel(x_ref, g_ref, sh_ref, sc_ref, w_ref, o_ref, hn_ref, *, tm):
    @pl.when(pl.program_id(1) == 0)
    def _():
        g, sh, sc = g_ref[...], sh_ref[...], sc_ref[...]
        rc = 128
        for r in range(tm // rc):
            hn_ref[r * rc:(r + 1) * rc, :] = _norm_mod(x_ref[r * rc:(r + 1) * rc, :], g, sh, sc).astype(BF16)

    o_ref[...] = jnp.dot(hn_ref[...], w_ref[...], preferred_element_type=F32)


def _proj_shift_kernel(x_ref, xp_ref, xn_ref, g_ref, sh_ref, sc_ref, w_ref, cw_ref, o_ref, hn_ref, *, tm,
                       plain_blocks):
    i, j = pl.program_id(0), pl.program_id(1)

    @pl.when(j == 0)
    def _():
        _fill_hn(hn_ref, x_ref, xp_ref, xn_ref, g_ref, sh_ref, sc_ref, tm, i == 0, i == pl.num_programs(0) - 1)

    z = jnp.dot(hn_ref[...], w_ref[...], preferred_element_type=F32)
    plain = functools.reduce(jnp.logical_or, [j == b for b in plain_blocks])

    @pl.when(plain)
    def _():
        o_ref[...] = z[HALO:HALO + tm, :]

    @pl.when(jnp.logical_not(plain))
    def _():
        o_ref[...] = _conv3(z, cw_ref[...], tm)


def _halo_specs(t, tm, d):
    per = tm // HALO
    nblk = t // HALO
    return [pl.BlockSpec((tm, d), lambda i, j: (i, 0)),
            pl.BlockSpec((HALO, d), lambda i, j: (jnp.maximum(i * per - 1, 0), 0)),
            pl.BlockSpec((HALO, d), lambda i, j: (jnp.minimum((i + 1) * per, nblk - 1), 0))]


def _vec_spec(d):
    return pl.BlockSpec((1, d), lambda i, j: (0, 0))


def _pick_tm(t):
    return 512 if t % 512 == 0 else 256


def _proj(x, g, sh, sc, w, conv_w=None, plain_cols=()):
    t, d = x.shape
    n = w.shape[1]
    tm, tn = _pick_tm(t), 512
    grid = (t // tm, n // tn)
    wspec = pl.BlockSpec((d, tn), lambda i, j: (0, j))
    ospec = pl.BlockSpec((tm, tn), lambda i, j: (i, j))
    oshape = jax.ShapeDtypeStruct((t, n), F32)
    if conv_w is None:
        return pl.pallas_call(
            functools.partial(_proj_kernel, tm=tm), grid=grid,
            in_specs=[pl.BlockSpec((tm, d), lambda i, j: (i, 0)), _vec_spec(d), _vec_spec(d), _vec_spec(d), wspec],
            out_specs=ospec, out_shape=oshape,
            scratch_shapes=[pltpu.VMEM((tm, d), BF16)],
            compiler_params=_cparams("parallel", "arbitrary"), name="proj",
        )(x, g, sh, sc, w)
    plain_blocks = tuple(b for b in range(n // tn)
                         if any(lo <= b * tn and (b + 1) * tn <= hi for lo, hi in plain_cols))
    assert plain_blocks, "expected at least one identity-tap column block"
    return pl.pallas_call(
        functools.partial(_proj_shift_kernel, tm=tm, plain_blocks=plain_blocks), grid=grid,
        in_specs=_halo_specs(t, tm, d) + [_vec_spec(d), _vec_spec(d), _vec_spec(d), wspec,
                                         pl.BlockSpec((3, tn), lambda i, j: (0, j))],
        out_specs=ospec, out_shape=oshape,
        scratch_shapes=[pltpu.VMEM((tm + 2 * HALO, d), BF16)],
        compiler_params=_cparams("parallel", "arbitrary"), name="proj_shift",
    )(x, x, x, g, sh, sc, w, conv_w)


def _ffn_kernel(x_ref, xp_ref, xn_ref, g_ref, sh_ref, sc_ref, gate_ref, fg_ref, wg_ref, wv_ref, cwg_ref, cwv_ref,
                cbg_ref, cbv_ref, wo_ref, o_ref, hn_ref, acc_ref, *, tm, final_norm):
    i, j = pl.program_id(0), pl.program_id(1)

    @pl.when(j == 0)
    def _():
        _fill_hn(hn_ref, x_ref, xp_ref, xn_ref, g_ref, sh_ref, sc_ref, tm, i == 0, i == pl.num_programs(0) - 1)
        acc_ref[...] = jnp.zeros_like(acc_ref)

    hn = hn_ref[...]
    gate = _conv3(jnp.dot(hn, wg_ref[...], preferred_element_type=F32), cwg_ref[...], tm) + cbg_ref[...]
    val = _conv3(jnp.dot(hn, wv_ref[...], preferred_element_type=F32), cwv_ref[...], tm) + cbv_ref[...]
    act = (gate * _sigmoid(gate) * val).astype(BF16)
    acc_ref[...] += jnp.dot(act, wo_ref[...], preferred_element_type=F32)

    @pl.when(j == pl.num_programs(1) - 1)
    def _():
        rc = 128
        for r in range(tm // rc):
            rows = slice(r * rc, (r + 1) * rc)
            y = x_ref[rows, :] + gate_ref[...] * acc_ref[rows, :]
            if final_norm:
                y = y * lax.rsqrt(jnp.mean(y * y, axis=-1, keepdims=True) + EPS) * fg_ref[...]
            o_ref[rows, :] = y


def _ffn(x, g, sh, sc, gate, w_in, conv_w, conv_b, w_out, final_g=None):
    t, d = x.shape
    f = w_out.shape[0]
    tm, tn = _pick_tm(t), 512
    nf = f // tn
    final_norm = final_g is not None
    fg = final_g if final_norm else g
    return pl.pallas_call(
        functools.partial(_ffn_kernel, tm=tm, final_norm=final_norm), grid=(t // tm, nf),
        in_specs=_halo_specs(t, tm, d) + [
            _vec_spec(d), _vec_spec(d), _vec_spec(d), _vec_spec(d), _vec_spec(d),
            pl.BlockSpec((d, tn), lambda i, j: (0, j)),
            pl.BlockSpec((d, tn), lambda i, j: (0, j + nf)),
            pl.BlockSpec((3, tn), lambda i, j: (0, j)),
            pl.BlockSpec((3, tn), lambda i, j: (0, j + nf)),
            pl.BlockSpec((1, tn), lambda i, j: (0, j)),
            pl.BlockSpec((1, tn), lambda i, j: (0, j + nf)),
            pl.BlockSpec((tn, d), lambda i, j: (j, 0))],
        out_specs=pl.BlockSpec((tm, d), lambda i, j: (i, 0)),
        out_shape=jax.ShapeDtypeStruct((t, d), F32),
        scratch_shapes=[pltpu.VMEM((tm + 2 * HALO, d), BF16), pltpu.VMEM((tm, d), F32)],
        compiler_params=_cparams("parallel", "arbitrary"), name="conv_ffn",
    )(x, x, x, g, sh, sc, gate, fg, w_in, w_in, conv_w, conv_w, conv_b, conv_b, w_out)


def _outproj_kernel(a_ref, b_ref, x_ref, gate_ref, wa_ref, wb_ref, o_ref):
    mix = jnp.dot(a_ref[...], wa_ref[...], preferred_element_type=F32)
    mix = mix + jnp.dot(b_ref[...], wb_ref[...], preferred_element_type=F32)
    o_ref[...] = x_ref[...] + gate_ref[...] * mix


def _outproj(a, b, x, gate, wa, wb):
    t, d = x.shape
    tm = _pick_tm(t)
    return pl.pallas_call(
        _outproj_kernel, grid=(t // tm,),
        in_specs=[pl.BlockSpec((tm, MIX_W), lambda i: (i, 0)),
                  pl.BlockSpec((tm, MIX_W), lambda i: (i, 0)),
                  pl.BlockSpec((tm, d), lambda i: (i, 0)),
                  pl.BlockSpec((1, d), lambda i: (0, 0)),
                  pl.BlockSpec((MIX_W, d), lambda i: (0, 0)),
                  pl.BlockSpec((MIX_W, d), lambda i: (0, 0))],
        out_specs=pl.BlockSpec((tm, d), lambda i: (i, 0)),
        out_shape=jax.ShapeDtypeStruct((t, d), F32),
        compiler_params=_cparams("parallel"), name="out_proj",
    )(a, b, x, gate, wa, wb)


def _rope(x, cos, sin):
    lane = lax.broadcasted_iota(jnp.int32, x.shape, 1)
    up = pltpu.roll(x, 32, 1)
    dn = pltpu.roll(x, HEAD_DIM - 32, 1)
    rot = jnp.where((lane % 64) < 32, -dn, up)
    return x * cos + rot * sin


def _qk_kernel(q_ref, k_ref, v_ref, cos_ref, sin_ref, qg_ref, kg_ref, qo_ref, ko_ref, vo_ref, *, norm):
    cos, sin = cos_ref[...], sin_ref[...]

    def prep(xh, g, scale):
        if norm:
            xh = xh * lax.rsqrt(jnp.mean(xh * xh, axis=-1, keepdims=True) + EPS) * g
        xh = _rope(xh, cos, sin)
        return (xh * scale).astype(BF16) if scale != 1.0 else xh.astype(BF16)

    for h in range(N_Q_HEADS):
        sl = slice(h * HEAD_DIM, (h + 1) * HEAD_DIM)
        qo_ref[:, sl] = prep(q_ref[:, sl], qg_ref[...], Q_SCALE)
    for h in range(N_KV_HEADS):
        sl = slice(h * HEAD_DIM, (h + 1) * HEAD_DIM)
        ko_ref[:, sl] = prep(k_ref[:, sl], kg_ref[...], 1.0)
        vo_ref[:, h * V_AUG:h * V_AUG + HEAD_DIM] = v_ref[:, sl].astype(BF16)
        vo_ref[:, h * V_AUG + HEAD_DIM:(h + 1) * V_AUG] = jnp.ones((v_ref.shape[0], HEAD_DIM), BF16)


def _qk_prep(p, cos, sin, qg, kg, q_off, k_off, v_off, norm):
    t = p.shape[0]
    tm = 256
    return pl.pallas_call(
        functools.partial(_qk_kernel, norm=norm), grid=(t // tm,),
        in_specs=[pl.BlockSpec((tm, MIX_W), lambda i: (i, q_off // MIX_W)),
                  pl.BlockSpec((tm, KV_W), lambda i: (i, k_off // KV_W)),
                  pl.BlockSpec((tm, KV_W), lambda i: (i, v_off // KV_W)),
                  pl.BlockSpec((tm, HEAD_DIM), lambda i: (i, 0)),
                  pl.BlockSpec((tm, HEAD_DIM), lambda i: (i, 0)),
                  pl.BlockSpec((1, HEAD_DIM), lambda i: (0, 0)),
                  pl.BlockSpec((1, HEAD_DIM), lambda i: (0, 0))],
        out_specs=[pl.BlockSpec((tm, MIX_W), lambda i: (i, 0)),
                   pl.BlockSpec((tm, KV_W), lambda i: (i, 0)),
                   pl.BlockSpec((tm, N_KV_HEADS * V_AUG), lambda i: (i, 0))],
        out_shape=[jax.ShapeDtypeStruct((t, MIX_W), BF16),
                   jax.ShapeDtypeStruct((t, KV_W), BF16),
                   jax.ShapeDtypeStruct((t, N_KV_HEADS * V_AUG), BF16)],
        compiler_params=_cparams("parallel"), name="qk_prep",
    )(p, p, p, cos, sin, qg, kg)


def _flash_kernel(sink_ref, q_ref, k_ref, v_ref, o_ref, qs_ref, m_ref, acc_ref, *, tq, use_sink):
    g, ki = pl.program_id(0), pl.program_id(2)

    @pl.when(ki == 0)
    def _():
        for h in range(GQA_GROUP):
            qs_ref[h * tq:(h + 1) * tq, :] = q_ref[:, h * HEAD_DIM:(h + 1) * HEAD_DIM]
            if use_sink:
                m_ref[h * tq:(h + 1) * tq, :] = jnp.full((tq, 1), sink_ref[g * GQA_GROUP + h] * LOG2E, F32)
        acc_ref[:, :HEAD_DIM] = jnp.zeros((GQA_GROUP * tq, HEAD_DIM), F32)
        if use_sink:
            acc_ref[:, HEAD_DIM:] = jnp.ones((GQA_GROUP * tq, HEAD_DIM), F32)
        else:
            m_ref[...] = jnp.full_like(m_ref, -jnp.inf)
            acc_ref[:, HEAD_DIM:] = jnp.zeros((GQA_GROUP * tq, HEAD_DIM), F32)

    k, v = k_ref[...], v_ref[...]

    def scores(h):
        return lax.dot_general(qs_ref[h * tq:(h + 1) * tq, :], k, _NT, preferred_element_type=F32)

    def weigh(h, s):
        rows = slice(h * tq, (h + 1) * tq)
        m_prev = m_ref[rows, :]
        m_new = jnp.maximum(m_prev, jnp.max(s, axis=-1, keepdims=True))
        m_ref[rows, :] = m_new
        p = jnp.exp2(s - m_new).astype(BF16)
        return jnp.exp2(m_prev - m_new), jnp.dot(p, v, preferred_element_type=F32)

    s_next = scores(0)
    upd = []
    for h in range(GQA_GROUP):
        s_cur = s_next
        if h + 1 < GQA_GROUP:
            s_next = scores(h + 1)
        upd.append(weigh(h, s_cur))
    for h, (alpha, pv) in enumerate(upd):
        rows = slice(h * tq, (h + 1) * tq)
        acc_ref[rows, :] = alpha * acc_ref[rows, :] + pv

    @pl.when(ki == pl.num_programs(2) - 1)
    def _():
        out = acc_ref[:, :HEAD_DIM] / acc_ref[:, HEAD_DIM:]
        for h in range(GQA_GROUP):
            o_ref[:, h * HEAD_DIM:(h + 1) * HEAD_DIM] = out[h * tq:(h + 1) * tq, :].astype(BF16)


def _pick_tk(tk_total):
    for c in (1280, 1024, 512, 256):
        if tk_total % c == 0:
            return c
    return 128


def _dense_gqa(q, k, v, sink=None):
    tq_total, tk_total = q.shape[0], k.shape[0]
    tq = 512 if tq_total % 512 == 0 else 256
    tk = _pick_tk(tk_total)
    use_sink = sink is not None
    sink_arr = sink if use_sink else jnp.zeros((N_Q_HEADS,), F32)
    gw = GQA_GROUP * HEAD_DIM
    return pl.pallas_call(
        functools.partial(_flash_kernel, tq=tq, use_sink=use_sink),
        grid=(N_KV_HEADS, tq_total // tq, tk_total // tk),
        in_specs=[pl.BlockSpec(memory_space=pltpu.SMEM),
                  pl.BlockSpec((tq, gw), lambda g, qi, ki: (qi, g)),
                  pl.BlockSpec((tk, HEAD_DIM), lambda g, qi, ki: (ki, g)),
                  pl.BlockSpec((tk, V_AUG), lambda g, qi, ki: (ki, g))],
        out_specs=pl.BlockSpec((tq, gw), lambda g, qi, ki: (qi, g)),
        out_shape=jax.ShapeDtypeStruct((tq_total, MIX_W), BF16),
        scratch_shapes=[pltpu.VMEM((GQA_GROUP * tq, HEAD_DIM), BF16),
                        pltpu.VMEM((GQA_GROUP * tq, 1), F32),
                        pltpu.VMEM((GQA_GROUP * tq, V_AUG), F32)],
        compiler_params=_cparams("parallel", "parallel", "arbitrary"), name="dense_gqa",
    )(sink_arr, q, k, v)


def _window_kernel(sink_ref, q_ref, kp_ref, k0_ref, kx_ref, kc_ref, vp_ref, v0_ref, vx_ref, vc_ref, o_ref, *, tq):
    g, n = pl.program_id(0), pl.program_id(1)
    nq = pl.num_programs(1)
    band = tq + 2 * BLK
    kcat = jnp.concatenate([kp_ref[...], k0_ref[...], kx_ref[...], kc_ref[...]], axis=0)
    vcat = jnp.concatenate([vp_ref[...], v0_ref[...], vx_ref[...], vc_ref[...]], axis=0)
    shape = (tq, kcat.shape[0])
    row = lax.broadcasted_iota(jnp.int32, shape, 0)
    col = lax.broadcasted_iota(jnp.int32, shape, 1)
    rel = col - BLK - row
    valid = (jnp.abs(rel) <= BLK) & ((col >= BLK) | (n > 0)) & ((col < BLK + tq) | (n < nq - 1))
    valid = valid | (col >= band)

    def scores(h):
        return lax.dot_general(q_ref[:, h * HEAD_DIM:(h + 1) * HEAD_DIM], kcat, _NT, preferred_element_type=F32)

    s_next = scores(0)
    for h in range(GQA_GROUP):
        s = jnp.where(valid, s_next, -jnp.inf)
        if h + 1 < GQA_GROUP:
            s_next = scores(h + 1)
        sink = sink_ref[g * GQA_GROUP + h] * LOG2E
        m = jnp.maximum(jnp.max(s, axis=-1, keepdims=True), sink)
        p = jnp.exp2(s - m).astype(BF16)
        pv = jnp.dot(p, vcat, preferred_element_type=F32)
        out = pv[:, :HEAD_DIM] / (pv[:, HEAD_DIM:] + jnp.exp2(sink - m))
        o_ref[:, h * HEAD_DIM:(h + 1) * HEAD_DIM] = out.astype(BF16)


def _window_gqa(q, k, v, kc, vc, sink):
    t = q.shape[0]
    tq = 256 if t % 256 == 0 else BLK
    per = tq // BLK
    nb = t // BLK
    tc = kc.shape[0]
    gw = GQA_GROUP * HEAD_DIM
    prev = lambda g, n: (jnp.maximum(n * per - 1, 0), g)
    cur = lambda g, n: (n, g)
    nxt = lambda g, n: (jnp.minimum((n + 1) * per, nb - 1), g)
    ctx = lambda g, n: (0, g)
    kspecs = [pl.BlockSpec((BLK, HEAD_DIM), prev), pl.BlockSpec((tq, HEAD_DIM), cur),
              pl.BlockSpec((BLK, HEAD_DIM), nxt), pl.BlockSpec((tc, HEAD_DIM), ctx)]
    vspecs = [pl.BlockSpec((BLK, V_AUG), prev), pl.BlockSpec((tq, V_AUG), cur),
              pl.BlockSpec((BLK, V_AUG), nxt), pl.BlockSpec((tc, V_AUG), ctx)]
    return pl.pallas_call(
        functools.partial(_window_kernel, tq=tq), grid=(N_KV_HEADS, t // tq),
        in_specs=[pl.BlockSpec(memory_space=pltpu.SMEM), pl.BlockSpec((tq, gw), lambda g, n: (n, g))]
        + kspecs + vspecs,
        out_specs=pl.BlockSpec((tq, gw), lambda g, n: (n, g)),
        out_shape=jax.ShapeDtypeStruct((t, MIX_W), BF16),
        compiler_params=_cparams("parallel", "parallel"), name="window_gqa",
    )(sink, q, k, k, k, kc, v, v, v, vc)


def _sgu_kernel(u_ref, z_ref, ng_ref, ws_ref, bs_ref, o_ref, *, chunks):
    for c in range(chunks):
        rows = slice(c * BLK, (c + 1) * BLK)
        for g in range(SGU_GROUPS):
            cols = slice(g * 128, (g + 1) * 128)
            z = _gelu_tanh(z_ref[rows, cols])
            mu = jnp.mean(z, axis=-1, keepdims=True)
            zc = z - mu
            var = jnp.mean(zc * zc, axis=-1, keepdims=True)
            vn = zc * lax.rsqrt(var + EPS) * ng_ref[:, cols]
            vm = jnp.dot(ws_ref[g], vn.astype(BF16), preferred_element_type=F32) + bs_ref[g]
            o_ref[rows, cols] = (_gelu_tanh(u_ref[rows, cols]) * vm).astype(BF16)


def _sgu(p, norm_g, w_s, b_s):
    t = p.shape[0]
    chunks = 2
    tm = chunks * BLK
    bs = jnp.broadcast_to(b_s[:, :, None], (SGU_GROUPS, BLK, 128))
    return pl.pallas_call(
        functools.partial(_sgu_kernel, chunks=chunks), grid=(t // tm,),
        in_specs=[pl.BlockSpec((tm, MIX_W), lambda i: (i, EV_U // MIX_W)),
                  pl.BlockSpec((tm, MIX_W), lambda i: (i, EV_Z // MIX_W)),
                  pl.BlockSpec((1, MIX_W), lambda i: (0, 0)),
                  pl.BlockSpec((SGU_GROUPS, BLK, BLK), lambda i: (0, 0, 0)),
                  pl.BlockSpec((SGU_GROUPS, BLK, 128), lambda i: (0, 0, 0))],
        out_specs=pl.BlockSpec((tm, MIX_W), lambda i: (i, 0)),
        out_shape=jax.ShapeDtypeStruct((t, MIX_W), BF16),
        compiler_params=_cparams("parallel"), name="chunk_sgu",
    )(p, p, norm_g.reshape(1, MIX_W), w_s.astype(BF16), bs)


def _split3(x):
    hi = x.astype(BF16)
    r1 = x - hi.astype(F32)
    mid = r1.astype(BF16)
    lo = (r1 - mid.astype(F32)).astype(BF16)
    return hi, mid, lo


def _seg_sum(x, e):
    parts = []
    for b in range(x.shape[1] // 128):
        hi, mid, lo = _split3(x[:, b * 128:(b + 1) * 128])
        parts.append(jnp.dot(hi, e, preferred_element_type=F32) + jnp.dot(mid, e, preferred_element_type=F32)
                     + jnp.dot(lo, e, preferred_element_type=F32))
    return jnp.concatenate(parts, axis=1)


def _run_sum(tri, x):
    hi, mid, lo = _split3(x)
    return (jnp.dot(tri, hi, preferred_element_type=F32) + jnp.dot(tri, mid, preferred_element_type=F32)
            + jnp.dot(tri, lo, preferred_element_type=F32))


def _feat_kernel(r_ref, k_ref, v_ref, gl_ref, wl_ref, al_ref, g2_ref, w2_ref, a2_ref, w0_ref, a0_ref,
                 kk_ref, ka_ref, rk_ref, e_ref, tri_ref, ld_ref, cum_ref, kd_ref, ad_ref, kko_ref, bon_ref, go_ref):
    e = e_ref[...]
    r, k, v = r_ref[...], k_ref[...], v_ref[...]
    go_ref[...] = jnp.dot(_sigmoid(gl_ref[...]), g2_ref[...], preferred_element_type=F32)
    lw = w0_ref[...] + jnp.dot(jnp.tanh(wl_ref[...]), w2_ref[...], preferred_element_type=F32)
    nl = -lw
    softplus = jnp.maximum(nl, 0.0) + jnp.log(1.0 + jnp.exp(-jnp.abs(nl)))
    ld = -jnp.exp(-softplus - 0.5)
    ld_ref[...] = ld
    cum_ref[:, :MIX_W] = _run_sum(tri_ref[0], ld[:, :MIX_W])
    cum_ref[:, MIX_W:] = _run_sum(tri_ref[1], ld[:, MIX_W:])
    a = _sigmoid(a0_ref[...] + jnp.dot(al_ref[...], a2_ref[...], preferred_element_type=F32))
    ad_ref[...] = a
    kk = k * kk_ref[...]
    nrm = jnp.maximum(jnp.sqrt(_seg_sum(kk * kk, e)), 1e-12)
    kko_ref[...] = kk / nrm
    ka = ka_ref[...]
    kd0 = k * (1.0 + (a[:, :MIX_W] - 1.0) * ka)
    kd1 = k * (1.0 + (a[:, MIX_W:] - 1.0) * ka)
    kd_ref[:, :MIX_W] = kd0
    kd_ref[:, MIX_W:] = kd1
    bon_ref[...] = _seg_sum(r * (kd0 + kd1) * rk_ref[...], e) * v


def _rwkv_features(zs, g2p, w2b, a2b, w0, a0, k_k, k_a, r_k, e):
    t = zs.shape[0]
    tm = 256
    col = lambda w, off: pl.BlockSpec((tm, w), lambda i: (i, off // w))
    full = lambda a: pl.BlockSpec(a.shape, lambda i: (0,) * a.ndim)
    wide = lambda w: pl.BlockSpec((tm, w), lambda i: (i, 0))
    row = jnp.arange(tm)
    same_chunk = (row[:, None] // SCAN_CHUNK) == (row[None, :] // SCAN_CHUNK)
    tri = jnp.stack([same_chunk & (row[None, :] <= row[:, None]),
                     same_chunk & (row[None, :] >= row[:, None])]).astype(BF16)
    params = [g2p, w2b, a2b, w0, a0, k_k, k_a, r_k, e, tri]
    return pl.pallas_call(
        _feat_kernel, grid=(t // tm,),
        in_specs=[col(MIX_W, OD_R), col(MIX_W, OD_K), col(MIX_W, OD_V), col(D_GATE_PAD, OD_G),
                  col(128, OD_W), col(128, OD_A)] + [full(a) for a in params],
        out_specs=[wide(2 * MIX_W)] * 4 + [wide(MIX_W)] * 3,
        out_shape=[jax.ShapeDtypeStruct((t, 2 * MIX_W), F32)] * 4 + [jax.ShapeDtypeStruct((t, MIX_W), F32)] * 3,
        compiler_params=_cparams("parallel"), name="rwkv_features",
    )(zs, zs, zs, zs, zs, zs, *params)


def _bdot(a, b, dims=(((1,), (0,)), ((), ()))):
    return lax.dot_general(a.astype(BF16), b.astype(BF16), dims, preferred_element_type=F32)


def _scan_masks():
    L = SCAN_CHUNK
    r = jnp.arange(2 * L)[:, None]
    c = jnp.arange(2 * L)[None, :]
    same = (r // L) == (c // L)
    out = []
    for reverse in (False, True):
        before = (c > r) if reverse else (c < r)
        out += [same & before, same & (before | (c == r))]
        b = 1
        while b < L:
            late_r, late_c = (r // b) % 2 == 1, (c // b) % 2 == 1
            couple = (~late_r & late_c) if reverse else (late_r & ~late_c)
            out.append(((r // (2 * b)) == (c // (2 * b))) & couple)
            b *= 2
    return jnp.stack(out).astype(F32), same.astype(F32)


def _chunk_group(streams, mask_ref, nlv, same, eye, m0, m1):
    L = SCAN_CHUNK
    n = len(streams)
    pre = []
    for ld, cum, kd, ad, r, v, kk, s_prev, d in streams:
        g_inc = jnp.exp(cum)
        g_exc = jnp.exp(cum - ld)
        g_inv = jnp.exp(-cum)
        last = 0 if d else L - 1
        g_tot = g_inc[last:last + 1, :]
        at = -kk * g_exc
        rt = r * g_inc
        bt = (kk * ad * g_inv).astype(BF16)
        kt = (kd * g_inv).astype(BF16)
        lar = jnp.concatenate([at * m0, at * m1, rt * m0, rt * m1], axis=0).astype(BF16)
        vbd = jnp.concatenate([v * m0, v * m1], axis=0).astype(BF16)
        pre.append((lar, bt, kt, vbd, v.astype(BF16), g_tot, s_prev, d * nlv))
    gram = [_bdot(lar, jnp.concatenate([bt, bt, kt, kt], axis=0), _NT) for lar, bt, kt, _, _, _, _, _ in pre]
    xs = [_bdot(p[0], p[6], _NT) for p in pre]
    n_ab, m_ak, m_r = [], [], []
    for g, p in zip(gram, pre):
        strict, incl = mask_ref[p[7]] > 0, mask_ref[p[7] + 1] > 0
        n_ab.append(jnp.where(strict, g[:2 * L, :2 * L], 0.0))
        m_ak.append(jnp.where(strict, g[:2 * L, 2 * L:], 0.0).astype(BF16))
        m_r.append(jnp.concatenate([jnp.where(incl, g[2 * L:, :2 * L], 0.0),
                                    jnp.where(incl, g[2 * L:, 2 * L:], 0.0)], axis=1).astype(BF16))
    x = [xs[i][:2 * L] + _bdot(m_ak[i], pre[i][3]) for i in range(n)]
    tinv = [eye + n_ab[i] * mask_ref[pre[i][7] + 2] for i in range(n)]
    for q in range(3, nlv):
        tb = [t.astype(BF16) for t in tinv]
        half = [_bdot(tb[i], n_ab[i] * mask_ref[pre[i][7] + q]) for i in range(n)]
        tinv = [tinv[i] + _bdot(half[i], tb[i]) for i in range(n)]
    ubd = [_bdot(tinv[i], x[i]) for i in range(n)]
    ybd = [xs[i][2 * L:] + _bdot(m_r[i], jnp.concatenate([ubd[i].astype(BF16), pre[i][3]], axis=0)) for i in range(n)]
    out = []
    for i in range(n):
        _, bt, kt, _, vb, g_tot, s_prev, _ = pre[i]
        u = (ubd[i][:L, :] + ubd[i][L:, :]).astype(BF16)
        upd = _bdot(jnp.concatenate([u, vb], axis=0), jnp.concatenate([bt, kt], axis=0), _TN)
        out.append((ybd[i][:L, :] + ybd[i][L:, :], (s_prev + upd * same) * g_tot))
    return out


def _scan_kernel(ldf_ref, cumf_ref, kdf_ref, adf_ref, rf_ref, vf_ref, kkf_ref,
                 ldb_ref, cumb_ref, kdb_ref, adb_ref, rb_ref, vb_ref, kkb_ref,
                 s0f_ref, s0b_ref, mask_ref, same_ref, eye_ref,
                 yf_ref, yb_ref, sff_ref, sfb_ref, sf_ref, sb_ref):
    L = SCAN_CHUNK
    c = pl.program_id(0)

    @pl.when(c == 0)
    def _():
        sf_ref[...] = s0f_ref[...]
        sb_ref[...] = s0b_ref[...]

    lane = lax.broadcasted_iota(jnp.int32, (L, 2 * RWKV_HEAD), 1)
    m0 = (lane < RWKV_HEAD).astype(F32)
    m1 = 1.0 - m0
    nlv = mask_ref.shape[0] // 2
    same, eye = same_ref[...], eye_ref[...]
    dirs = ((ldf_ref, cumf_ref, kdf_ref, adf_ref, rf_ref, vf_ref, kkf_ref, sf_ref, yf_ref, 0),
            (ldb_ref, cumb_ref, kdb_ref, adb_ref, rb_ref, vb_ref, kkb_ref, sb_ref, yb_ref, 1))
    for p0 in range(0, RWKV_PAIRS, SCAN_GROUP_PAIRS):
        streams, dests = [], []
        for p in range(p0, p0 + SCAN_GROUP_PAIRS):
            cols = slice(p * 128, (p + 1) * 128)
            for ld_ref, cum_ref, kd_ref, ad_ref, r_ref, v_ref, kk_ref, s_ref, y_ref, d in dirs:
                streams.append((ld_ref[:, cols], cum_ref[:, cols], kd_ref[:, cols], ad_ref[:, cols],
                                r_ref[:, cols], v_ref[:, cols], kk_ref[:, cols], s_ref[p], d))
                dests.append((y_ref, s_ref, p, cols))
        for (y_ref, s_ref, p, cols), (y, s_new) in zip(dests, _chunk_group(streams, mask_ref, nlv, same, eye, m0, m1)):
            y_ref[:, cols] = y
            s_ref[p] = s_new

    @pl.when(c == pl.num_programs(0) - 1)
    def _():
        sff_ref[...] = sf_ref[...]
        sfb_ref[...] = sb_ref[...]


def _rwkv_scan(ld, cum, kd, ad, zs, kk, s0f, s0b):
    t = zs.shape[0]
    L = SCAN_CHUNK
    nc = t // L
    masks, same = _scan_masks()
    eye = jnp.eye(2 * L, dtype=F32)
    specs = []
    for d, cidx in ((0, lambda c: c), (1, lambda c: nc - 1 - c)):
        dirblk = pl.BlockSpec((L, MIX_W), lambda c, cidx=cidx, d=d: (cidx(c), d))
        specs += [dirblk, dirblk, dirblk, dirblk,
                  pl.BlockSpec((L, MIX_W), lambda c, cidx=cidx: (cidx(c), OD_R // MIX_W)),
                  pl.BlockSpec((L, MIX_W), lambda c, cidx=cidx: (cidx(c), OD_V // MIX_W)),
                  pl.BlockSpec((L, MIX_W), lambda c, cidx=cidx: (cidx(c), 0))]
    sblk = pl.BlockSpec((RWKV_PAIRS, 128, 128), lambda c: (0, 0, 0))
    const = lambda a: pl.BlockSpec(a.shape, lambda c: (0,) * a.ndim)
    feats = (ld, cum, kd, ad, zs, zs, kk)
    return pl.pallas_call(
        _scan_kernel, grid=(nc,),
        in_specs=specs + [sblk, sblk, const(masks), const(same), const(eye)],
        out_specs=[pl.BlockSpec((L, MIX_W), lambda c: (c, 0)),
                   pl.BlockSpec((L, MIX_W), lambda c: (nc - 1 - c, 0)), sblk, sblk],
        out_shape=[jax.ShapeDtypeStruct((t, MIX_W), F32)] * 2 + [jax.ShapeDtypeStruct((RWKV_PAIRS, 128, 128), F32)] * 2,
        scratch_shapes=[pltpu.VMEM((RWKV_PAIRS, 128, 128), F32)] * 2,
        compiler_params=_cparams("arbitrary"), name="rwkv_scan",
    )(*feats, *feats, s0f, s0b, masks, same, eye)


def _readout_kernel(yf_ref, yb_ref, bon_ref, g_ref, lw_ref, lb_ref, e_ref, o_ref):
    e = e_ref[...]
    y = yf_ref[...] + yb_ref[...]
    mu = _seg_sum(y, e) * (1.0 / RWKV_HEAD)
    yc = y - mu
    var = _seg_sum(yc * yc, e) * (1.0 / RWKV_HEAD)
    yn = yc * lax.rsqrt(var + GN_EPS) * lw_ref[...] + lb_ref[...]
    o_ref[...] = ((yn + bon_ref[...]) * g_ref[...]).astype(BF16)


def _rwkv_readout(yf, yb, bonus, g, ln_w, ln_b, e):
    t = yf.shape[0]
    tm = 256
    wide = pl.BlockSpec((tm, MIX_W), lambda i: (i, 0))
    vec = pl.BlockSpec((1, MIX_W), lambda i: (0, 0))
    return pl.pallas_call(
        _readout_kernel, grid=(t // tm,),
        in_specs=[wide, wide, wide, wide, vec, vec, pl.BlockSpec((128, 128), lambda i: (0, 0))],
        out_specs=wide, out_shape=jax.ShapeDtypeStruct((t, MIX_W), BF16),
        compiler_params=_cparams("parallel"), name="rwkv_readout",
    )(yf, yb, bonus, g, ln_w, ln_b, e)


def _rope_tables(rows):
    half = HEAD_DIM // 2
    inv = ROPE_THETA ** (-jnp.arange(0, half, 2, dtype=F32) / half)
    row = jnp.repeat(jnp.arange(rows, dtype=F32), GRID_W)
    col = jnp.tile(jnp.arange(GRID_W, dtype=F32), rows)
    ang_r = row[:, None] * inv[None, :]
    ang_c = col[:, None] * inv[None, :]
    ang = jnp.concatenate([ang_r, ang_r, ang_c, ang_c], axis=-1)
    return jnp.cos(ang), jnp.sin(ang)


def _even_layer(x, ctx, mx, mc, n1g, w_in, w_out, qg, kg, vng, w_s, b_s, tabs, with_ctx):
    o_k = N_Q_HEADS * HEAD_DIM
    o_v = o_k + KV_W
    o_u = o_v + KV_W
    o_z = o_u + MIX_W
    wp = jnp.concatenate([w_in[:, :o_k], w_in[:, o_u:o_z], w_in[:, o_z:], w_in[:, o_k:o_v], w_in[:, o_v:o_u]],
                         axis=1).astype(BF16)
    wa, wb = w_out[:MIX_W].astype(BF16), w_out[MIX_W:].astype(BF16)
    qg, kg = qg.reshape(1, HEAD_DIM), kg.reshape(1, HEAD_DIM)
    (cos_x, sin_x), (cos_c, sin_c) = tabs
    px = _proj(x, n1g, mx[0], mx[1], wp)
    pc = _proj(ctx, n1g, mc[0], mc[1], wp)
    qx, kx, vx = _qk_prep(px, cos_x, sin_x, qg, kg, EV_Q, EV_K, EV_V, True)
    qc, kc, vc = _qk_prep(pc, cos_c, sin_c, qg, kg, EV_Q, EV_K, EV_V, True)
    ax = _dense_gqa(qx, jnp.concatenate([kx, kc], axis=0), jnp.concatenate([vx, vc], axis=0))
    bx = _sgu(px, vng, w_s, b_s)
    x = _outproj(ax, bx, x, mx[2], wa, wb)
    if with_ctx:
        ac = _dense_gqa(qc, kc, vc)
        bc = _sgu(pc, vng, w_s, b_s)
        ctx = _outproj(ac, bc, ctx, mc[2], wa, wb)
    return x, ctx


def _odd_layer(x, ctx, mx, mc, n1g, w_in, w_out, sink, shift_mu, w0, w2, a0, a2, g2, k_k, k_a, r_k, ln_w, ln_b,
               tabs, with_ctx):
    o_k = N_Q_HEADS * HEAD_DIM
    o_v = o_k + KV_W
    c_in = o_v + KV_W
    o_g = 3 * MIX_W
    o_w = o_g + D_GATE_LORA
    o_a = o_w + 2 * D_LORA
    d = w_in.shape[0]
    wr = w_in[:, c_in:]
    gpad = jnp.zeros((d, D_GATE_PAD - D_GATE_LORA), w_in.dtype)
    wp = jnp.concatenate([w_in[:, :o_k], wr[:, :o_g], w_in[:, o_k:o_v], w_in[:, o_v:c_in],
                          wr[:, o_g:o_w], gpad, wr[:, o_w:o_a], wr[:, o_a:]], axis=1).astype(BF16)
    taps = jnp.stack([shift_mu[0], 1.0 - shift_mu[0] - shift_mu[1], shift_mu[1]], axis=0)
    ident = jnp.tile(jnp.array([[0.0], [1.0], [0.0]], F32), (1, 1))
    cw = jnp.concatenate([jnp.tile(ident, (1, MIX_W)), taps[:, :o_g], jnp.tile(ident, (1, 2 * KV_W)),
                          taps[:, o_g:o_w], jnp.tile(ident, (1, D_GATE_PAD - D_GATE_LORA)),
                          taps[:, o_w:o_a], taps[:, o_a:]], axis=1)
    wa, wb = w_out[:MIX_W].astype(BF16), w_out[MIX_W:].astype(BF16)
    g2p = jnp.concatenate([g2, jnp.zeros((D_GATE_PAD - D_GATE_LORA, MIX_W), g2.dtype)], axis=0)
    zl = jnp.zeros((D_LORA, MIX_W), F32)
    w2b = jnp.concatenate([jnp.concatenate([w2[0], zl], axis=1), jnp.concatenate([zl, w2[1]], axis=1)], axis=0)
    a2b = jnp.concatenate([jnp.concatenate([a2[0], zl], axis=1), jnp.concatenate([zl, a2[1]], axis=1)], axis=0)
    w0f, a0f = w0.reshape(1, 2 * MIX_W), a0.reshape(1, 2 * MIX_W)
    kkv, kav, rkv = k_k.reshape(1, MIX_W), k_a.reshape(1, MIX_W), r_k.reshape(1, MIX_W)
    lnw, lnb = ln_w.reshape(1, MIX_W), ln_b.reshape(1, MIX_W)
    lane = jnp.arange(128)
    e = (lane[:, None] // RWKV_HEAD == lane[None, :] // RWKV_HEAD).astype(BF16)
    ones_g = jnp.ones((1, HEAD_DIM), F32)
    (cos_x, sin_x), (cos_c, sin_c) = tabs

    plain = ((OD_Q, OD_R), (OD_AK, OD_G))
    px = _proj(x, n1g, mx[0], mx[1], wp, cw, plain)
    pc = _proj(ctx, n1g, mc[0], mc[1], wp, cw, plain)
    qx, kx, vx = _qk_prep(px, cos_x, sin_x, ones_g, ones_g, OD_Q, OD_AK, OD_AV, False)
    qc, kc, vc = _qk_prep(pc, cos_c, sin_c, ones_g, ones_g, OD_Q, OD_AK, OD_AV, False)
    cx = _window_gqa(qx, kx, vx, kc, vc, sink)

    fparams = (g2p, w2b, a2b, w0f, a0f, kkv, kav, rkv, e)
    ld_x, cum_x, kd_x, ad_x, kk_x, bon_x, g_x = _rwkv_features(px, *fparams)
    ld_c, cum_c, kd_c, ad_c, kk_c, bon_c, g_c = _rwkv_features(pc, *fparams)
    s0 = jnp.zeros((RWKV_PAIRS, 128, 128), F32)
    y_cf, y_cb, s_cf, s_cb = _rwkv_scan(ld_c, cum_c, kd_c, ad_c, pc, kk_c, s0, s0)
    y_xf, y_xb, _, _ = _rwkv_scan(ld_x, cum_x, kd_x, ad_x, px, kk_x, s_cf, s_cb)
    dx = _rwkv_readout(y_xf, y_xb, bon_x, g_x, lnw, lnb, e)
    x = _outproj(cx, dx, x, mx[2], wa, wb)
    if with_ctx:
        cc = _dense_gqa(qc, kc, vc, sink)
        dc = _rwkv_readout(y_cf, y_cb, bon_c, g_c, lnw, lnb, e)
        ctx = _outproj(cc, dc, ctx, mc[2], wa, wb)
    return x, ctx


def kernel(x, c, ctx, c_ctx, ada_w, ada_b, norm1_g, norm2_g, ffn_w_in, ffn_conv_w, ffn_conv_b, ffn_w_out,
           ev_w_in, ev_w_out, a_q_norm_g, a_k_norm_g, b_v_norm_g, b_spatial_w, b_spatial_b,
           od_w_in, od_w_out, c_sink, d_shift_mu, d_w0, d_w2, d_a0, d_a2, d_g2, d_k_k, d_k_a, d_r_k,
           d_ln_w, d_ln_b, final_norm_g):
    bsz, t, d = x.shape
    assert bsz == 1 and d == D_MODEL and ada_w.shape[0] >= 1
    tc = ctx.shape[1]
    depth = ada_w.shape[0]
    xs, cs = x[0], ctx[0]
    tabs = (_rope_tables(t // GRID_W), (jnp.ones((tc, HEAD_DIM), F32), jnp.zeros((tc, HEAD_DIM), F32)))
    sv = jnp.stack([jax.nn.silu(c[0]), jax.nn.silu(c_ctx)], axis=1)
    mods = _ada_mod(sv, ada_w, ada_b)
    for l in range(depth):
        with_ctx = l < depth - 1
        mx = [mods[l, 0, k * d:(k + 1) * d].reshape(1, d) for k in range(6)]
        mc = [mods[l, 1, k * d:(k + 1) * d].reshape(1, d) for k in range(6)]
        n1g, n2g = norm1_g[l].reshape(1, d), norm2_g[l].reshape(1, d)
        i = l // 2
        if l % 2 == 0:
            xs, cs = _even_layer(xs, cs, mx, mc, n1g, ev_w_in[i], ev_w_out[i], a_q_norm_g[i], a_k_norm_g[i],
                                 b_v_norm_g[i], b_spatial_w[i], b_spatial_b[i], tabs, with_ctx)
        else:
            xs, cs = _odd_layer(xs, cs, mx, mc, n1g, od_w_in[i], od_w_out[i], c_sink[i], d_shift_mu[i], d_w0[i],
                                d_w2[i], d_a0[i], d_a2[i], d_g2[i], d_k_k[i], d_k_a[i], d_r_k[i], d_ln_w[i],
                                d_ln_b[i], tabs, with_ctx)
        fw_in, fw_out = ffn_w_in[l].astype(BF16), ffn_w_out[l].astype(BF16)
        fcb = ffn_conv_b[l].reshape(1, 2 * D_FF)
        final_g = final_norm_g.reshape(1, d) if l == depth - 1 else None
        xs = _ffn(xs, n2g, mx[3], mx[4], mx[5], fw_in, ffn_conv_w[l], fcb, fw_out, final_g)
        if with_ctx:
            cs = _ffn(cs, n2g, mc[3], mc[4], mc[5], fw_in, ffn_conv_w[l], fcb, fw_out)
    return xs[None]
```

```python
import functools
import math

import jax
import jax.numpy as jnp
from jax import lax
from jax.experimental import pallas as pl
from jax.experimental.pallas import tpu as pltpu

F32 = jnp.float32
BF16 = jnp.bfloat16

D_MODEL = 2048
GRID_W = 64
HEAD_DIM = 128
BLK = 128
MIX_W = D_MODEL // 2
N_Q_HEADS = MIX_W // HEAD_DIM
N_KV_HEADS = N_Q_HEADS // 4
GQA_GROUP = N_Q_HEADS // N_KV_HEADS
KV_W = N_KV_HEADS * HEAD_DIM
SGU_GROUPS = MIX_W // 128
RWKV_HEAD = 64
RWKV_PAIRS = MIX_W // (2 * RWKV_HEAD)
D_GATE_LORA = 160
D_GATE_PAD = 256
D_LORA = 64
D_FF = 5632
ROPE_THETA = 10000.0
EPS = 1e-6
GN_EPS = 64e-5
LOG2E = math.log2(math.e)
Q_SCALE = HEAD_DIM ** -0.5 * LOG2E
V_AUG = 2 * HEAD_DIM

HALO = 8
SCAN_CHUNK = 64
SCAN_GROUP_PAIRS = 8
VMEM_LIMIT = 56 * 1024 * 1024

EV_Q, EV_U, EV_Z, EV_K, EV_V = 0, 1024, 2048, 3072, 3328
EV_N = 3584
OD_Q, OD_R, OD_K, OD_V, OD_AK, OD_AV, OD_G, OD_W, OD_A = 0, 1024, 2048, 3072, 4096, 4352, 4608, 4864, 4992
OD_N = 5120


_NT = (((1,), (1,)), ((), ()))
_TN = (((0,), (0,)), ((), ()))


def _cparams(*sem):
    return pltpu.CompilerParams(dimension_semantics=sem, vmem_limit_bytes=VMEM_LIMIT)


def _sigmoid(x):
    return 1.0 / (1.0 + jnp.exp(-x))


def _gelu_tanh(x):
    return 0.5 * x * (1.0 + jnp.tanh(math.sqrt(2.0 / math.pi) * (x + 0.044715 * (x * x * x))))


def _norm_mod(xv, g, sh, sc):
    ms = jnp.mean(xv * xv, axis=-1, keepdims=True)
    y = xv * lax.rsqrt(ms + EPS) * g
    return y * (1.0 + sc) + sh


def _ada_kernel(sv_ref, w_ref, b_ref, o_ref):
    kc = 256
    nk = w_ref.shape[1] // kc
    tn = w_ref.shape[2]

    def body(c, acc):
        a0, a1 = acc
        k0 = pl.multiple_of(c * kc, kc)
        w = w_ref[0, pl.ds(k0, kc), :]
        s = sv_ref[pl.ds(k0, kc), :]
        a0 = a0 + jnp.sum(w * s[:, 0:1], axis=0, keepdims=True)
        a1 = a1 + jnp.sum(w * s[:, 1:2], axis=0, keepdims=True)
        return a0, a1

    z = jnp.zeros((1, tn), F32)
    a0, a1 = lax.fori_loop(0, nk, body, (z, z))
    b = b_ref[0]
    o_ref[0] = jnp.concatenate([a0 + b, a1 + b], axis=0)


def _ada_mod(sv, ada_w, ada_b):
    depth, d, n = ada_w.shape
    tn = 1024
    return pl.pallas_call(
        _ada_kernel,
        grid=(depth, n // tn),
        in_specs=[pl.BlockSpec((d, 2), lambda l, j: (0, 0)),
                  pl.BlockSpec((1, d, tn), lambda l, j: (l, 0, j)),
                  pl.BlockSpec((1, 1, tn), lambda l, j: (l, 0, j))],
        out_specs=pl.BlockSpec((1, 2, tn), lambda l, j: (l, 0, j)),
        out_shape=jax.ShapeDtypeStruct((depth, 2, n), F32),
        compiler_params=_cparams("parallel", "parallel"),
        name="ada_mod",
    )(sv, ada_w, ada_b.reshape(depth, 1, n))


def _fill_hn(hn_ref, x_ref, xp_ref, xn_ref, g_ref, sh_ref, sc_ref, tm, first, last):
    g, sh, sc = g_ref[...], sh_ref[...], sc_ref[...]
    rc = 128
    for r in range(tm // rc):
        hn_ref[r * rc:(r + 1) * rc, :] = _norm_mod(x_ref[r * rc:(r + 1) * rc, :], g, sh, sc).astype(BF16)
    hx = jnp.where(last, 0.0, _norm_mod(xn_ref[...], g, sh, sc))
    hp = jnp.where(first, 0.0, _norm_mod(xp_ref[...], g, sh, sc))
    hn_ref[tm:tm + 2 * HALO, :] = jnp.concatenate([hx, hp], axis=0).astype(BF16)


def _conv3(z, cw, tm):
    rows = z.shape[0]
    zp = pltpu.roll(z, 1, 0)
    zn = pltpu.roll(z, rows - 1, 0)
    out = cw[0:1, :] * zp + cw[1:2, :] * z + cw[2:3, :] * zn
    return out[:tm, :]


def _proj_kernel(x_ref, g_ref, sh_ref, sc_ref, w_ref, o_ref, hn_ref, *, tm):
    @pl.when(pl.program_id(1) == 0)
    def _():
        g, sh, sc = g_ref[...], sh_ref[...], sc_ref[...]
        rc = 128
        for r in range(tm // rc):
            hn_ref[r * rc:(r + 1) * rc, :] = _norm_mod(x_ref[r * rc:(r + 1) * rc, :], g, sh, sc).astype(BF16)

    o_ref[...] = jnp.dot(hn_ref[...], w_ref[...], preferred_element_type=F32)


def _proj_shift_kernel(x_ref, xp_ref, xn_ref, g_ref, sh_ref, sc_ref, w_ref, cw_ref, o_ref, hn_ref, *, tm,
                       plain_blocks):
    i, j = pl.program_id(0), pl.program_id(1)

    @pl.when(j == 0)
    def _():
        _fill_hn(hn_ref, x_ref, xp_ref, xn_ref, g_ref, sh_ref, sc_ref, tm, i == 0, i == pl.num_programs(0) - 1)

    z = jnp.dot(hn_ref[...], w_ref[...], preferred_element_type=F32)
    plain = functools.reduce(jnp.logical_or, [j == b for b in plain_blocks])

    @pl.when(plain)
    def _():
        o_ref[...] = z[:tm, :]

    @pl.when(jnp.logical_not(plain))
    def _():
        o_ref[...] = _conv3(z, cw_ref[...], tm)


def _halo_specs(t, tm, d):
    per = tm // HALO
    nblk = t // HALO
    return [pl.BlockSpec((tm, d), lambda i, j: (i, 0)),
            pl.BlockSpec((HALO, d), lambda i, j: (jnp.maximum(i * per - 1, 0), 0)),
            pl.BlockSpec((HALO, d), lambda i, j: (jnp.minimum((i + 1) * per, nblk - 1), 0))]


def _vec_spec(d):
    return pl.BlockSpec((1, d), lambda i, j: (0, 0))


def _pick_tm(t):
    return 512 if t % 512 == 0 else 256


def _proj(x, g, sh, sc, w, conv_w=None, plain_cols=()):
    t, d = x.shape
    n = w.shape[1]
    tm = _pick_tm(t)
    tn = next(c for c in (1792, 1024, 512) if n % c == 0)
    grid = (t // tm, n // tn)
    wspec = pl.BlockSpec((d, tn), lambda i, j: (0, j))
    ospec = pl.BlockSpec((tm, tn), lambda i, j: (i, j))
    oshape = jax.ShapeDtypeStruct((t, n), F32)
    if conv_w is None:
        return pl.pallas_call(
            functools.partial(_proj_kernel, tm=tm), grid=grid,
            in_specs=[pl.BlockSpec((tm, d), lambda i, j: (i, 0)), _vec_spec(d), _vec_spec(d), _vec_spec(d), wspec],
            out_specs=ospec, out_shape=oshape,
            scratch_shapes=[pltpu.VMEM((tm, d), BF16)],
            compiler_params=_cparams("parallel", "arbitrary"), name="proj",
        )(x, g, sh, sc, w)
    plain_blocks = tuple(b for b in range(n // tn)
                         if any(lo <= b * tn and (b + 1) * tn <= hi for lo, hi in plain_cols))
    assert plain_blocks, "expected at least one identity-tap column block"
    return pl.pallas_call(
        functools.partial(_proj_shift_kernel, tm=tm, plain_blocks=plain_blocks), grid=grid,
        in_specs=_halo_specs(t, tm, d) + [_vec_spec(d), _vec_spec(d), _vec_spec(d), wspec,
                                         pl.BlockSpec((3, tn), lambda i, j: (0, j))],
        out_specs=ospec, out_shape=oshape,
        scratch_shapes=[pltpu.VMEM((tm + 2 * HALO, d), BF16)],
        compiler_params=_cparams("parallel", "arbitrary"), name="proj_shift",
    )(x, x, x, g, sh, sc, w, conv_w)


def _ffn_kernel(x_ref, xp_ref, xn_ref, g_ref, sh_ref, sc_ref, gate_ref, fg_ref, wg_ref, wv_ref, cwg_ref, cwv_ref,
                cbg_ref, cbv_ref, wo_ref, o_ref, hn_ref, acc_ref, *, tm, final_norm):
    i, j = pl.program_id(0), pl.program_id(1)

    @pl.when(j == 0)
    def _():
        _fill_hn(hn_ref, x_ref, xp_ref, xn_ref, g_ref, sh_ref, sc_ref, tm, i == 0, i == pl.num_programs(0) - 1)
        acc_ref[...] = jnp.zeros_like(acc_ref)

    hn = hn_ref[...]
    gate = _conv3(jnp.dot(hn, wg_ref[...], preferred_element_type=F32), cwg_ref[...], tm) + cbg_ref[...]
    val = _conv3(jnp.dot(hn, wv_ref[...], preferred_element_type=F32), cwv_ref[...], tm) + cbv_ref[...]
    act = (gate * _sigmoid(gate) * val).astype(BF16)
    acc_ref[...] += jnp.dot(act, wo_ref[...], preferred_element_type=F32)

    @pl.when(j == pl.num_programs(1) - 1)
    def _():
        rc = 128
        for r in range(tm // rc):
            rows = slice(r * rc, (r + 1) * rc)
            y = x_ref[rows, :] + gate_ref[...] * acc_ref[rows, :]
            if final_norm:
                y = y * lax.rsqrt(jnp.mean(y * y, axis=-1, keepdims=True) + EPS) * fg_ref[...]
            o_ref[rows, :] = y


def _ffn(x, g, sh, sc, gate, w_in, conv_w, conv_b, w_out, final_g=None):
    t, d = x.shape
    f = w_out.shape[0]
    tm, tn = _pick_tm(t), 512
    nf = f // tn
    final_norm = final_g is not None
    fg = final_g if final_norm else g
    return pl.pallas_call(
        functools.partial(_ffn_kernel, tm=tm, final_norm=final_norm), grid=(t // tm, nf),
        in_specs=_halo_specs(t, tm, d) + [
            _vec_spec(d), _vec_spec(d), _vec_spec(d), _vec_spec(d), _vec_spec(d),
            pl.BlockSpec((d, tn), lambda i, j: (0, j)),
            pl.BlockSpec((d, tn), lambda i, j: (0, j + nf)),
            pl.BlockSpec((3, tn), lambda i, j: (0, j)),
            pl.BlockSpec((3, tn), lambda i, j: (0, j + nf)),
            pl.BlockSpec((1, tn), lambda i, j: (0, j)),
            pl.BlockSpec((1, tn), lambda i, j: (0, j + nf)),
            pl.BlockSpec((tn, d), lambda i, j: (j, 0))],
        out_specs=pl.BlockSpec((tm, d), lambda i, j: (i, 0)),
        out_shape=jax.ShapeDtypeStruct((t, d), F32),
        scratch_shapes=[pltpu.VMEM((tm + 2 * HALO, d), BF16), pltpu.VMEM((tm, d), F32)],
        compiler_params=_cparams("parallel", "arbitrary"), name="conv_ffn",
    )(x, x, x, g, sh, sc, gate, fg, w_in, w_in, conv_w, conv_w, conv_b, conv_b, w_out)


def _outproj_kernel(a_ref, b_ref, x_ref, gate_ref, wa_ref, wb_ref, o_ref):
    mix = jnp.dot(a_ref[...], wa_ref[...], preferred_element_type=F32)
    mix = mix + jnp.dot(b_ref[...], wb_ref[...], preferred_element_type=F32)
    o_ref[...] = x_ref[...] + gate_ref[...] * mix


def _outproj(a, b, x, gate, wa, wb):
    t, d = x.shape
    tm = _pick_tm(t)
    return pl.pallas_call(
        _outproj_kernel, grid=(t // tm,),
        in_specs=[pl.BlockSpec((tm, MIX_W), lambda i: (i, 0)),
                  pl.BlockSpec((tm, MIX_W), lambda i: (i, 0)),
                  pl.BlockSpec((tm, d), lambda i: (i, 0)),
                  pl.BlockSpec((1, d), lambda i: (0, 0)),
                  pl.BlockSpec((MIX_W, d), lambda i: (0, 0)),
                  pl.BlockSpec((MIX_W, d), lambda i: (0, 0))],
        out_specs=pl.BlockSpec((tm, d), lambda i: (i, 0)),
        out_shape=jax.ShapeDtypeStruct((t, d), F32),
        compiler_params=_cparams("parallel"), name="out_proj",
    )(a, b, x, gate, wa, wb)


def _rope(x, cos, sin):
    lane = lax.broadcasted_iota(jnp.int32, x.shape, 1)
    up = pltpu.roll(x, 32, 1)
    dn = pltpu.roll(x, HEAD_DIM - 32, 1)
    rot = jnp.where((lane % 64) < 32, -dn, up)
    return x * cos + rot * sin


def _qk_kernel(q_ref, k_ref, v_ref, cos_ref, sin_ref, qg_ref, kg_ref, qo_ref, ko_ref, vo_ref, *, norm):
    cos, sin = cos_ref[...], sin_ref[...]

    def prep(xh, g, scale):
        if norm:
            xh = xh * lax.rsqrt(jnp.mean(xh * xh, axis=-1, keepdims=True) + EPS) * g
        xh = _rope(xh, cos, sin)
        return (xh * scale).astype(BF16) if scale != 1.0 else xh.astype(BF16)

    for h in range(N_Q_HEADS):
        sl = slice(h * HEAD_DIM, (h + 1) * HEAD_DIM)
        qo_ref[:, sl] = prep(q_ref[:, sl], qg_ref[...], Q_SCALE)
    for h in range(N_KV_HEADS):
        sl = slice(h * HEAD_DIM, (h + 1) * HEAD_DIM)
        ko_ref[:, sl] = prep(k_ref[:, sl], kg_ref[...], 1.0)
        vo_ref[:, h * V_AUG:h * V_AUG + HEAD_DIM] = v_ref[:, sl].astype(BF16)
        vo_ref[:, h * V_AUG + HEAD_DIM:(h + 1) * V_AUG] = jnp.ones((v_ref.shape[0], HEAD_DIM), BF16)


def _qk_prep(p, cos, sin, qg, kg, q_off, k_off, v_off, norm):
    t = p.shape[0]
    tm = 256
    return pl.pallas_call(
        functools.partial(_qk_kernel, norm=norm), grid=(t // tm,),
        in_specs=[pl.BlockSpec((tm, MIX_W), lambda i: (i, q_off // MIX_W)),
                  pl.BlockSpec((tm, KV_W), lambda i: (i, k_off // KV_W)),
                  pl.BlockSpec((tm, KV_W), lambda i: (i, v_off // KV_W)),
                  pl.BlockSpec((tm, HEAD_DIM), lambda i: (i, 0)),
                  pl.BlockSpec((tm, HEAD_DIM), lambda i: (i, 0)),
                  pl.BlockSpec((1, HEAD_DIM), lambda i: (0, 0)),
                  pl.BlockSpec((1, HEAD_DIM), lambda i: (0, 0))],
        out_specs=[pl.BlockSpec((tm, MIX_W), lambda i: (i, 0)),
                   pl.BlockSpec((tm, KV_W), lambda i: (i, 0)),
                   pl.BlockSpec((tm, N_KV_HEADS * V_AUG), lambda i: (i, 0))],
        out_shape=[jax.ShapeDtypeStruct((t, MIX_W), BF16),
                   jax.ShapeDtypeStruct((t, KV_W), BF16),
                   jax.ShapeDtypeStruct((t, N_KV_HEADS * V_AUG), BF16)],
        compiler_params=_cparams("parallel"), name="qk_prep",
    )(p, p, p, cos, sin, qg, kg)


def _flash_kernel(sink_ref, q_ref, k_ref, v_ref, o_ref, qs_ref, m_ref, acc_ref, *, tq, use_sink):
    g, ki = pl.program_id(0), pl.program_id(2)

    @pl.when(ki == 0)
    def _():
        for h in range(GQA_GROUP):
            qs_ref[h * tq:(h + 1) * tq, :] = q_ref[:, h * HEAD_DIM:(h + 1) * HEAD_DIM]
            if use_sink:
                m_ref[h * tq:(h + 1) * tq, :] = jnp.full((tq, 1), sink_ref[g * GQA_GROUP + h] * LOG2E, F32)
        acc_ref[:, :HEAD_DIM] = jnp.zeros((GQA_GROUP * tq, HEAD_DIM), F32)
        if use_sink:
            acc_ref[:, HEAD_DIM:] = jnp.ones((GQA_GROUP * tq, HEAD_DIM), F32)
        else:
            m_ref[...] = jnp.full_like(m_ref, -jnp.inf)
            acc_ref[:, HEAD_DIM:] = jnp.zeros((GQA_GROUP * tq, HEAD_DIM), F32)

    k, v = k_ref[...], v_ref[...]

    def scores(h):
        return lax.dot_general(qs_ref[h * tq:(h + 1) * tq, :], k, _NT, preferred_element_type=F32)

    def weigh(h, s):
        rows = slice(h * tq, (h + 1) * tq)
        m_prev = m_ref[rows, :]
        m_new = jnp.maximum(m_prev, jnp.max(s, axis=-1, keepdims=True))
        m_ref[rows, :] = m_new
        p = jnp.exp2(s - m_new).astype(BF16)
        return jnp.exp2(m_prev - m_new), jnp.dot(p, v, preferred_element_type=F32)

    s_next = scores(0)
    upd = []
    for h in range(GQA_GROUP):
        s_cur = s_next
        if h + 1 < GQA_GROUP:
            s_next = scores(h + 1)
        upd.append(weigh(h, s_cur))
    for h, (alpha, pv) in enumerate(upd):
        rows = slice(h * tq, (h + 1) * tq)
        acc_ref[rows, :] = alpha * acc_ref[rows, :] + pv

    @pl.when(ki == pl.num_programs(2) - 1)
    def _():
        out = acc_ref[:, :HEAD_DIM] / acc_ref[:, HEAD_DIM:]
        for h in range(GQA_GROUP):
            o_ref[:, h * HEAD_DIM:(h + 1) * HEAD_DIM] = out[h * tq:(h + 1) * tq, :].astype(BF16)


def _pick_tk(tk_total):
    for c in (1280, 1024, 512, 256):
        if tk_total % c == 0:
            return c
    return 128


def _dense_gqa(q, k, v, sink=None):
    tq_total, tk_total = q.shape[0], k.shape[0]
    tq = 512 if tq_total % 512 == 0 else 256
    tk = _pick_tk(tk_total)
    use_sink = sink is not None
    sink_arr = sink if use_sink else jnp.zeros((N_Q_HEADS,), F32)
    gw = GQA_GROUP * HEAD_DIM
    return pl.pallas_call(
        functools.partial(_flash_kernel, tq=tq, use_sink=use_sink),
        grid=(N_KV_HEADS, tq_total // tq, tk_total // tk),
        in_specs=[pl.BlockSpec(memory_space=pltpu.SMEM),
                  pl.BlockSpec((tq, gw), lambda g, qi, ki: (qi, g)),
                  pl.BlockSpec((tk, HEAD_DIM), lambda g, qi, ki: (ki, g)),
                  pl.BlockSpec((tk, V_AUG), lambda g, qi, ki: (ki, g))],
        out_specs=pl.BlockSpec((tq, gw), lambda g, qi, ki: (qi, g)),
        out_shape=jax.ShapeDtypeStruct((tq_total, MIX_W), BF16),
        scratch_shapes=[pltpu.VMEM((GQA_GROUP * tq, HEAD_DIM), BF16),
                        pltpu.VMEM((GQA_GROUP * tq, 1), F32),
                        pltpu.VMEM((GQA_GROUP * tq, V_AUG), F32)],
        compiler_params=_cparams("parallel", "parallel", "arbitrary"), name="dense_gqa",
    )(sink_arr, q, k, v)


def _window_kernel(sink_ref, q_ref, kp_ref, k0_ref, kx_ref, kc_ref, vp_ref, v0_ref, vx_ref, vc_ref, o_ref, *, tq):
    g, n = pl.program_id(0), pl.program_id(1)
    nq = pl.num_programs(1)
    band = tq + 2 * BLK
    kcat = jnp.concatenate([kp_ref[...], k0_ref[...], kx_ref[...], kc_ref[...]], axis=0)
    vcat = jnp.concatenate([vp_ref[...], v0_ref[...], vx_ref[...], vc_ref[...]], axis=0)
    shape = (tq, kcat.shape[0])
    row = lax.broadcasted_iota(jnp.int32, shape, 0)
    col = lax.broadcasted_iota(jnp.int32, shape, 1)
    rel = col - BLK - row
    valid = (jnp.abs(rel) <= BLK) & ((col >= BLK) | (n > 0)) & ((col < BLK + tq) | (n < nq - 1))
    valid = valid | (col >= band)

    def scores(h):
        return lax.dot_general(q_ref[:, h * HEAD_DIM:(h + 1) * HEAD_DIM], kcat, _NT, preferred_element_type=F32)

    s_next = scores(0)
    for h in range(GQA_GROUP):
        s = jnp.where(valid, s_next, -jnp.inf)
        if h + 1 < GQA_GROUP:
            s_next = scores(h + 1)
        sink = sink_ref[g * GQA_GROUP + h] * LOG2E
        m = jnp.maximum(jnp.max(s, axis=-1, keepdims=True), sink)
        p = jnp.exp2(s - m).astype(BF16)
        pv = jnp.dot(p, vcat, preferred_element_type=F32)
        out = pv[:, :HEAD_DIM] / (pv[:, HEAD_DIM:] + jnp.exp2(sink - m))
        o_ref[:, h * HEAD_DIM:(h + 1) * HEAD_DIM] = out.astype(BF16)


def _window_gqa(q, k, v, kc, vc, sink):
    t = q.shape[0]
    tq = 256 if t % 256 == 0 else BLK
    per = tq // BLK
    nb = t // BLK
    tc = kc.shape[0]
    gw = GQA_GROUP * HEAD_DIM
    prev = lambda g, n: (jnp.maximum(n * per - 1, 0), g)
    cur = lambda g, n: (n, g)
    nxt = lambda g, n: (jnp.minimum((n + 1) * per, nb - 1), g)
    ctx = lambda g, n: (0, g)
    kspecs = [pl.BlockSpec((BLK, HEAD_DIM), prev), pl.BlockSpec((tq, HEAD_DIM), cur),
              pl.BlockSpec((BLK, HEAD_DIM), nxt), pl.BlockSpec((tc, HEAD_DIM), ctx)]
    vspecs = [pl.BlockSpec((BLK, V_AUG), prev), pl.BlockSpec((tq, V_AUG), cur),
              pl.BlockSpec((BLK, V_AUG), nxt), pl.BlockSpec((tc, V_AUG), ctx)]
    return pl.pallas_call(
        functools.partial(_window_kernel, tq=tq), grid=(N_KV_HEADS, t // tq),
        in_specs=[pl.BlockSpec(memory_space=pltpu.SMEM), pl.BlockSpec((tq, gw), lambda g, n: (n, g))]
        + kspecs + vspecs,
        out_specs=pl.BlockSpec((tq, gw), lambda g, n: (n, g)),
        out_shape=jax.ShapeDtypeStruct((t, MIX_W), BF16),
        compiler_params=_cparams("parallel", "parallel"), name="window_gqa",
    )(sink, q, k, k, k, kc, v, v, v, vc)


def _sgu_kernel(u_ref, z_ref, ng_ref, ws_ref, bs_ref, o_ref, *, chunks):
    for c in range(chunks):
        rows = slice(c * BLK, (c + 1) * BLK)
        for g in range(SGU_GROUPS):
            cols = slice(g * 128, (g + 1) * 128)
            z = _gelu_tanh(z_ref[rows, cols])
            mu = jnp.mean(z, axis=-1, keepdims=True)
            zc = z - mu
            var = jnp.mean(zc * zc, axis=-1, keepdims=True)
            vn = zc * lax.rsqrt(var + EPS) * ng_ref[:, cols]
            vm = jnp.dot(ws_ref[g], vn.astype(BF16), preferred_element_type=F32) + bs_ref[g]
            o_ref[rows, cols] = (_gelu_tanh(u_ref[rows, cols]) * vm).astype(BF16)


def _sgu(p, norm_g, w_s, b_s):
    t = p.shape[0]
    chunks = 2
    tm = chunks * BLK
    bs = jnp.broadcast_to(b_s[:, :, None], (SGU_GROUPS, BLK, 128))
    return pl.pallas_call(
        functools.partial(_sgu_kernel, chunks=chunks), grid=(t // tm,),
        in_specs=[pl.BlockSpec((tm, MIX_W), lambda i: (i, EV_U // MIX_W)),
                  pl.BlockSpec((tm, MIX_W), lambda i: (i, EV_Z // MIX_W)),
                  pl.BlockSpec((1, MIX_W), lambda i: (0, 0)),
                  pl.BlockSpec((SGU_GROUPS, BLK, BLK), lambda i: (0, 0, 0)),
                  pl.BlockSpec((SGU_GROUPS, BLK, 128), lambda i: (0, 0, 0))],
        out_specs=pl.BlockSpec((tm, MIX_W), lambda i: (i, 0)),
        out_shape=jax.ShapeDtypeStruct((t, MIX_W), BF16),
        compiler_params=_cparams("parallel"), name="chunk_sgu",
    )(p, p, norm_g.reshape(1, MIX_W), w_s.astype(BF16), bs)


def _split3(x):
    hi = x.astype(BF16)
    r1 = x - hi.astype(F32)
    mid = r1.astype(BF16)
    lo = (r1 - mid.astype(F32)).astype(BF16)
    return hi, mid, lo


def _seg_sum(x, e):
    parts = []
    for b in range(x.shape[1] // 128):
        hi, mid, lo = _split3(x[:, b * 128:(b + 1) * 128])
        parts.append(jnp.dot(hi, e, preferred_element_type=F32) + jnp.dot(mid, e, preferred_element_type=F32)
                     + jnp.dot(lo, e, preferred_element_type=F32))
    return jnp.concatenate(parts, axis=1)


def _run_sum(tri, x):
    hi, mid, lo = _split3(x)
    return (jnp.dot(tri, hi, preferred_element_type=F32) + jnp.dot(tri, mid, preferred_element_type=F32)
            + jnp.dot(tri, lo, preferred_element_type=F32))


def _feat_kernel(r_ref, k_ref, v_ref, gl_ref, wl_ref, al_ref, g2_ref, w2_ref, a2_ref, w0_ref, a0_ref,
                 kk_ref, ka_ref, rk_ref, e_ref, tri_ref, ld_ref, cum_ref, kd_ref, ad_ref, kko_ref, bon_ref, go_ref):
    e = e_ref[...]
    r, k, v = r_ref[...], k_ref[...], v_ref[...]
    go_ref[...] = jnp.dot(_sigmoid(gl_ref[...]), g2_ref[...], preferred_element_type=F32)
    lw = w0_ref[...] + jnp.dot(jnp.tanh(wl_ref[...]), w2_ref[...], preferred_element_type=F32)
    nl = -lw
    softplus = jnp.maximum(nl, 0.0) + jnp.log(1.0 + jnp.exp(-jnp.abs(nl)))
    ld = -jnp.exp(-softplus - 0.5)
    ld_ref[...] = ld
    cum_ref[:, :MIX_W] = _run_sum(tri_ref[0], ld[:, :MIX_W])
    cum_ref[:, MIX_W:] = _run_sum(tri_ref[1], ld[:, MIX_W:])
    a = _sigmoid(a0_ref[...] + jnp.dot(al_ref[...], a2_ref[...], preferred_element_type=F32))
    ad_ref[...] = a
    kk = k * kk_ref[...]
    nrm = jnp.maximum(jnp.sqrt(_seg_sum(kk * kk, e)), 1e-12)
    kko_ref[...] = kk / nrm
    ka = ka_ref[...]
    kd0 = k * (1.0 + (a[:, :MIX_W] - 1.0) * ka)
    kd1 = k * (1.0 + (a[:, MIX_W:] - 1.0) * ka)
    kd_ref[:, :MIX_W] = kd0
    kd_ref[:, MIX_W:] = kd1
    bon_ref[...] = _seg_sum(r * (kd0 + kd1) * rk_ref[...], e) * v


def _rwkv_features(zs, g2p, w2b, a2b, w0, a0, k_k, k_a, r_k, e):
    t = zs.shape[0]
    tm = 256
    col = lambda w, off: pl.BlockSpec((tm, w), lambda i: (i, off // w))
    full = lambda a: pl.BlockSpec(a.shape, lambda i: (0,) * a.ndim)
    wide = lambda w: pl.BlockSpec((tm, w), lambda i: (i, 0))
    row = jnp.arange(tm)
    same_chunk = (row[:, None] // SCAN_CHUNK) == (row[None, :] // SCAN_CHUNK)
    tri = jnp.stack([same_chunk & (row[None, :] <= row[:, None]),
                     same_chunk & (row[None, :] >= row[:, None])]).astype(BF16)
    params = [g2p, w2b, a2b, w0, a0, k_k, k_a, r_k, e, tri]
    return pl.pallas_call(
        _feat_kernel, grid=(t // tm,),
        in_specs=[col(MIX_W, OD_R), col(MIX_W, OD_K), col(MIX_W, OD_V), col(D_GATE_PAD, OD_G),
                  col(128, OD_W), col(128, OD_A)] + [full(a) for a in params],
        out_specs=[wide(2 * MIX_W)] * 4 + [wide(MIX_W)] * 3,
        out_shape=[jax.ShapeDtypeStruct((t, 2 * MIX_W), F32)] * 4 + [jax.ShapeDtypeStruct((t, MIX_W), F32)] * 3,
        compiler_params=_cparams("parallel"), name="rwkv_features",
    )(zs, zs, zs, zs, zs, zs, *params)


def _bdot(a, b, dims=(((1,), (0,)), ((), ()))):
    return lax.dot_general(a.astype(BF16), b.astype(BF16), dims, preferred_element_type=F32)


def _scan_masks():
    L = SCAN_CHUNK
    r = jnp.arange(2 * L)[:, None]
    c = jnp.arange(2 * L)[None, :]
    same = (r // L) == (c // L)
    out = []
    for reverse in (False, True):
        before = (c > r) if reverse else (c < r)
        out += [same & before, same & (before | (c == r))]
        b = 1
        while b < L:
            late_r, late_c = (r // b) % 2 == 1, (c // b) % 2 == 1
            couple = (~late_r & late_c) if reverse else (late_r & ~late_c)
            out.append(((r // (2 * b)) == (c // (2 * b))) & couple)
            b *= 2
    return jnp.stack(out).astype(F32), same.astype(F32)


def _chunk_group(streams, mask_ref, nlv, same, eye, m0, m1):
    L = SCAN_CHUNK
    n = len(streams)
    pre = []
    for ld, cum, kd, ad, r, v, kk, s_prev, d in streams:
        g_inc = jnp.exp(cum)
        g_exc = jnp.exp(cum - ld)
        g_inv = jnp.exp(-cum)
        last = 0 if d else L - 1
        g_tot = g_inc[last:last + 1, :]
        at = -kk * g_exc
        rt = r * g_inc
        bt = (kk * ad * g_inv).astype(BF16)
        kt = (kd * g_inv).astype(BF16)
        lar = jnp.concatenate([at * m0, at * m1, rt * m0, rt * m1], axis=0).astype(BF16)
        vbd = jnp.concatenate([v * m0, v * m1], axis=0).astype(BF16)
        pre.append((lar, bt, kt, vbd, v.astype(BF16), g_tot, s_prev, d * nlv))
    gram = [_bdot(lar, jnp.concatenate([bt, bt, kt, kt], axis=0), _NT) for lar, bt, kt, _, _, _, _, _ in pre]
    xs = [_bdot(p[0], p[6], _NT) for p in pre]
    n_ab, m_ak, m_r = [], [], []
    for g, p in zip(gram, pre):
        strict, incl = mask_ref[p[7]] > 0, mask_ref[p[7] + 1] > 0
        n_ab.append(jnp.where(strict, g[:2 * L, :2 * L], 0.0))
        m_ak.append(jnp.where(strict, g[:2 * L, 2 * L:], 0.0).astype(BF16))
        m_r.append(jnp.concatenate([jnp.where(incl, g[2 * L:, :2 * L], 0.0),
                                    jnp.where(incl, g[2 * L:, 2 * L:], 0.0)], axis=1).astype(BF16))
    x = [xs[i][:2 * L] + _bdot(m_ak[i], pre[i][3]) for i in range(n)]
    tinv = [eye + n_ab[i] * mask_ref[pre[i][7] + 2] for i in range(n)]
    for q in range(3, nlv):
        tb = [t.astype(BF16) for t in tinv]
        half = [_bdot(tb[i], n_ab[i] * mask_ref[pre[i][7] + q]) for i in range(n)]
        tinv = [tinv[i] + _bdot(half[i], tb[i]) for i in range(n)]
    ubd = [_bdot(tinv[i], x[i]) for i in range(n)]
    ybd = [xs[i][2 * L:] + _bdot(m_r[i], jnp.concatenate([ubd[i].astype(BF16), pre[i][3]], axis=0)) for i in range(n)]
    out = []
    for i in range(n):
        _, bt, kt, _, vb, g_tot, s_prev, _ = pre[i]
        u = (ubd[i][:L, :] + ubd[i][L:, :]).astype(BF16)
        upd = _bdot(jnp.concatenate([u, vb], axis=0), jnp.concatenate([bt, kt], axis=0), _TN)
        out.append((ybd[i][:L, :] + ybd[i][L:, :], (s_prev + upd * same) * g_tot))
    return out


def _scan_kernel(ldf_ref, cumf_ref, kdf_ref, adf_ref, rf_ref, vf_ref, kkf_ref,
                 ldb_ref, cumb_ref, kdb_ref, adb_ref, rb_ref, vb_ref, kkb_ref,
                 s0f_ref, s0b_ref, mask_ref, same_ref, eye_ref,
                 yf_ref, yb_ref, sff_ref, sfb_ref, sf_ref, sb_ref):
    L = SCAN_CHUNK
    c = pl.program_id(0)

    @pl.when(c == 0)
    def _():
        sf_ref[...] = s0f_ref[...]
        sb_ref[...] = s0b_ref[...]

    lane = lax.broadcasted_iota(jnp.int32, (L, 2 * RWKV_HEAD), 1)
    m0 = (lane < RWKV_HEAD).astype(F32)
    m1 = 1.0 - m0
    nlv = mask_ref.shape[0] // 2
    same, eye = same_ref[...], eye_ref[...]
    dirs = ((ldf_ref, cumf_ref, kdf_ref, adf_ref, rf_ref, vf_ref, kkf_ref, sf_ref, yf_ref, 0),
            (ldb_ref, cumb_ref, kdb_ref, adb_ref, rb_ref, vb_ref, kkb_ref, sb_ref, yb_ref, 1))
    for p0 in range(0, RWKV_PAIRS, SCAN_GROUP_PAIRS):
        streams, dests = [], []
        for p in range(p0, p0 + SCAN_GROUP_PAIRS):
            cols = slice(p * 128, (p + 1) * 128)
            for ld_ref, cum_ref, kd_ref, ad_ref, r_ref, v_ref, kk_ref, s_ref, y_ref, d in dirs:
                streams.append((ld_ref[:, cols], cum_ref[:, cols], kd_ref[:, cols], ad_ref[:, cols],
                                r_ref[:, cols], v_ref[:, cols], kk_ref[:, cols], s_ref[p], d))
                dests.append((y_ref, s_ref, p, cols))
        for (y_ref, s_ref, p, cols), (y, s_new) in zip(dests, _chunk_group(streams, mask_ref, nlv, same, eye, m0, m1)):
            y_ref[:, cols] = y
            s_ref[p] = s_new

    @pl.when(c == pl.num_programs(0) - 1)
    def _():
        sff_ref[...] = sf_ref[...]
        sfb_ref[...] = sb_ref[...]


def _rwkv_scan(ld, cum, kd, ad, zs, kk, s0f, s0b):
    t = zs.shape[0]
    L = SCAN_CHUNK
    nc = t // L
    masks, same = _scan_masks()
    eye = jnp.eye(2 * L, dtype=F32)
    specs = []
    for d, cidx in ((0, lambda c: c), (1, lambda c: nc - 1 - c)):
        dirblk = pl.BlockSpec((L, MIX_W), lambda c, cidx=cidx, d=d: (cidx(c), d))
        specs += [dirblk, dirblk, dirblk, dirblk,
                  pl.BlockSpec((L, MIX_W), lambda c, cidx=cidx: (cidx(c), OD_R // MIX_W)),
                  pl.BlockSpec((L, MIX_W), lambda c, cidx=cidx: (cidx(c), OD_V // MIX_W)),
                  pl.BlockSpec((L, MIX_W), lambda c, cidx=cidx: (cidx(c), 0))]
    sblk = pl.BlockSpec((RWKV_PAIRS, 128, 128), lambda c: (0, 0, 0))
    const = lambda a: pl.BlockSpec(a.shape, lambda c: (0,) * a.ndim)
    feats = (ld, cum, kd, ad, zs, zs, kk)
    return pl.pallas_call(
        _scan_kernel, grid=(nc,),
        in_specs=specs + [sblk, sblk, const(masks), const(same), const(eye)],
        out_specs=[pl.BlockSpec((L, MIX_W), lambda c: (c, 0)),
                   pl.BlockSpec((L, MIX_W), lambda c: (nc - 1 - c, 0)), sblk, sblk],
        out_shape=[jax.ShapeDtypeStruct((t, MIX_W), F32)] * 2 + [jax.ShapeDtypeStruct((RWKV_PAIRS, 128, 128), F32)] * 2,
        scratch_shapes=[pltpu.VMEM((RWKV_PAIRS, 128, 128), F32)] * 2,
        compiler_params=_cparams("arbitrary"), name="rwkv_scan",
    )(*feats, *feats, s0f, s0b, masks, same, eye)


def _readout_kernel(yf_ref, yb_ref, bon_ref, g_ref, lw_ref, lb_ref, e_ref, o_ref):
    e = e_ref[...]
    y = yf_ref[...] + yb_ref[...]
    mu = _seg_sum(y, e) * (1.0 / RWKV_HEAD)
    yc = y - mu
    var = _seg_sum(yc * yc, e) * (1.0 / RWKV_HEAD)
    yn = yc * lax.rsqrt(var + GN_EPS) * lw_ref[...] + lb_ref[...]
    o_ref[...] = ((yn + bon_ref[...]) * g_ref[...]).astype(BF16)


def _rwkv_readout(yf, yb, bonus, g, ln_w, ln_b, e):
    t = yf.shape[0]
    tm = 256
    wide = pl.BlockSpec((tm, MIX_W), lambda i: (i, 0))
    vec = pl.BlockSpec((1, MIX_W), lambda i: (0, 0))
    return pl.pallas_call(
        _readout_kernel, grid=(t // tm,),
        in_specs=[wide, wide, wide, wide, vec, vec, pl.BlockSpec((128, 128), lambda i: (0, 0))],
        out_specs=wide, out_shape=jax.ShapeDtypeStruct((t, MIX_W), BF16),
        compiler_params=_cparams("parallel"), name="rwkv_readout",
    )(yf, yb, bonus, g, ln_w, ln_b, e)


def _rope_tables(rows):
    half = HEAD_DIM // 2
    inv = ROPE_THETA ** (-jnp.arange(0, half, 2, dtype=F32) / half)
    row = jnp.repeat(jnp.arange(rows, dtype=F32), GRID_W)
    col = jnp.tile(jnp.arange(GRID_W, dtype=F32), rows)
    ang_r = row[:, None] * inv[None, :]
    ang_c = col[:, None] * inv[None, :]
    ang = jnp.concatenate([ang_r, ang_r, ang_c, ang_c], axis=-1)
    return jnp.cos(ang), jnp.sin(ang)


def _even_layer(x, ctx, mx, mc, n1g, w_in, w_out, qg, kg, vng, w_s, b_s, tabs, with_ctx):
    o_k = N_Q_HEADS * HEAD_DIM
    o_v = o_k + KV_W
    o_u = o_v + KV_W
    o_z = o_u + MIX_W
    wp = jnp.concatenate([w_in[:, :o_k], w_in[:, o_u:o_z], w_in[:, o_z:], w_in[:, o_k:o_v], w_in[:, o_v:o_u]],
                         axis=1).astype(BF16)
    wa, wb = w_out[:MIX_W].astype(BF16), w_out[MIX_W:].astype(BF16)
    qg, kg = qg.reshape(1, HEAD_DIM), kg.reshape(1, HEAD_DIM)
    (cos_x, sin_x), (cos_c, sin_c) = tabs
    px = _proj(x, n1g, mx[0], mx[1], wp)
    pc = _proj(ctx, n1g, mc[0], mc[1], wp)
    qx, kx, vx = _qk_prep(px, cos_x, sin_x, qg, kg, EV_Q, EV_K, EV_V, True)
    qc, kc, vc = _qk_prep(pc, cos_c, sin_c, qg, kg, EV_Q, EV_K, EV_V, True)
    ax = _dense_gqa(qx, jnp.concatenate([kx, kc], axis=0), jnp.concatenate([vx, vc], axis=0))
    bx = _sgu(px, vng, w_s, b_s)
    x = _outproj(ax, bx, x, mx[2], wa, wb)
    if with_ctx:
        ac = _dense_gqa(qc, kc, vc)
        bc = _sgu(pc, vng, w_s, b_s)
        ctx = _outproj(ac, bc, ctx, mc[2], wa, wb)
    return x, ctx


def _odd_layer(x, ctx, mx, mc, n1g, w_in, w_out, sink, shift_mu, w0, w2, a0, a2, g2, k_k, k_a, r_k, ln_w, ln_b,
               tabs, with_ctx):
    o_k = N_Q_HEADS * HEAD_DIM
    o_v = o_k + KV_W
    c_in = o_v + KV_W
    o_g = 3 * MIX_W
    o_w = o_g + D_GATE_LORA
    o_a = o_w + 2 * D_LORA
    d = w_in.shape[0]
    wr = w_in[:, c_in:]
    gpad = jnp.zeros((d, D_GATE_PAD - D_GATE_LORA), w_in.dtype)
    wp = jnp.concatenate([w_in[:, :o_k], wr[:, :o_g], w_in[:, o_k:o_v], w_in[:, o_v:c_in],
                          wr[:, o_g:o_w], gpad, wr[:, o_w:o_a], wr[:, o_a:]], axis=1).astype(BF16)
    taps = jnp.stack([shift_mu[0], 1.0 - shift_mu[0] - shift_mu[1], shift_mu[1]], axis=0)
    ident = jnp.tile(jnp.array([[0.0], [1.0], [0.0]], F32), (1, 1))
    cw = jnp.concatenate([jnp.tile(ident, (1, MIX_W)), taps[:, :o_g], jnp.tile(ident, (1, 2 * KV_W)),
                          taps[:, o_g:o_w], jnp.tile(ident, (1, D_GATE_PAD - D_GATE_LORA)),
                          taps[:, o_w:o_a], taps[:, o_a:]], axis=1)
    wa, wb = w_out[:MIX_W].astype(BF16), w_out[MIX_W:].astype(BF16)
    g2p = jnp.concatenate([g2, jnp.zeros((D_GATE_PAD - D_GATE_LORA, MIX_W), g2.dtype)], axis=0)
    zl = jnp.zeros((D_LORA, MIX_W), F32)
    w2b = jnp.concatenate([jnp.concatenate([w2[0], zl], axis=1), jnp.concatenate([zl, w2[1]], axis=1)], axis=0)
    a2b = jnp.concatenate([jnp.concatenate([a2[0], zl], axis=1), jnp.concatenate([zl, a2[1]], axis=1)], axis=0)
    w0f, a0f = w0.reshape(1, 2 * MIX_W), a0.reshape(1, 2 * MIX_W)
    kkv, kav, rkv = k_k.reshape(1, MIX_W), k_a.reshape(1, MIX_W), r_k.reshape(1, MIX_W)
    lnw, lnb = ln_w.reshape(1, MIX_W), ln_b.reshape(1, MIX_W)
    lane = jnp.arange(128)
    e = (lane[:, None] // RWKV_HEAD == lane[None, :] // RWKV_HEAD).astype(BF16)
    ones_g = jnp.ones((1, HEAD_DIM), F32)
    (cos_x, sin_x), (cos_c, sin_c) = tabs

    plain = ((OD_Q, OD_R), (OD_AK, OD_G))
    px = _proj(x, n1g, mx[0], mx[1], wp, cw, plain)
    pc = _proj(ctx, n1g, mc[0], mc[1], wp, cw, plain)
    qx, kx, vx = _qk_prep(px, cos_x, sin_x, ones_g, ones_g, OD_Q, OD_AK, OD_AV, False)
    qc, kc, vc = _qk_prep(pc, cos_c, sin_c, ones_g, ones_g, OD_Q, OD_AK, OD_AV, False)
    cx = _window_gqa(qx, kx, vx, kc, vc, sink)

    fparams = (g2p, w2b, a2b, w0f, a0f, kkv, kav, rkv, e)
    ld_x, cum_x, kd_x, ad_x, kk_x, bon_x, g_x = _rwkv_features(px, *fparams)
    ld_c, cum_c, kd_c, ad_c, kk_c, bon_c, g_c = _rwkv_features(pc, *fparams)
    s0 = jnp.zeros((RWKV_PAIRS, 128, 128), F32)
    y_cf, y_cb, s_cf, s_cb = _rwkv_scan(ld_c, cum_c, kd_c, ad_c, pc, kk_c, s0, s0)
    y_xf, y_xb, _, _ = _rwkv_scan(ld_x, cum_x, kd_x, ad_x, px, kk_x, s_cf, s_cb)
    dx = _rwkv_readout(y_xf, y_xb, bon_x, g_x, lnw, lnb, e)
    x = _outproj(cx, dx, x, mx[2], wa, wb)
    if with_ctx:
        cc = _dense_gqa(qc, kc, vc, sink)
        dc = _rwkv_readout(y_cf, y_cb, bon_c, g_c, lnw, lnb, e)
        ctx = _outproj(cc, dc, ctx, mc[2], wa, wb)
    return x, ctx


def kernel(x, c, ctx, c_ctx, ada_w, ada_b, norm1_g, norm2_g, ffn_w_in, ffn_conv_w, ffn_conv_b, ffn_w_out,
           ev_w_in, ev_w_out, a_q_norm_g, a_k_norm_g, b_v_norm_g, b_spatial_w, b_spatial_b,
           od_w_in, od_w_out, c_sink, d_shift_mu, d_w0, d_w2, d_a0, d_a2, d_g2, d_k_k, d_k_a, d_r_k,
           d_ln_w, d_ln_b, final_norm_g):
    bsz, t, d = x.shape
    assert bsz == 1 and d == D_MODEL and ada_w.shape[0] >= 1
    tc = ctx.shape[1]
    depth = ada_w.shape[0]
    xs, cs = x.reshape(t, d), ctx.reshape(tc, d)
    tabs = (_rope_tables(t // GRID_W), (jnp.ones((tc, HEAD_DIM), F32), jnp.zeros((tc, HEAD_DIM), F32)))
    sv = jnp.stack([jax.nn.silu(c[0]), jax.nn.silu(c_ctx)], axis=1)
    mods = _ada_mod(sv, ada_w, ada_b)
    for l in range(depth):
        with_ctx = l < depth - 1
        mx = [mods[l, 0, k * d:(k + 1) * d].reshape(1, d) for k in range(6)]
        mc = [mods[l, 1, k * d:(k + 1) * d].reshape(1, d) for k in range(6)]
        n1g, n2g = norm1_g[l].reshape(1, d), norm2_g[l].reshape(1, d)
        i = l // 2
        if l % 2 == 0:
            xs, cs = _even_layer(xs, cs, mx, mc, n1g, ev_w_in[i], ev_w_out[i], a_q_norm_g[i], a_k_norm_g[i],
                                 b_v_norm_g[i], b_spatial_w[i], b_spatial_b[i], tabs, with_ctx)
        else:
            xs, cs = _odd_layer(xs, cs, mx, mc, n1g, od_w_in[i], od_w_out[i], c_sink[i], d_shift_mu[i], d_w0[i],
                                d_w2[i], d_a0[i], d_a2[i], d_g2[i], d_k_k[i], d_k_a[i], d_r_k[i], d_ln_w[i],
                                d_ln_b[i], tabs, with_ctx)
        fw_in, fw_out = ffn_w_in[l].astype(BF16), ffn_w_out[l].astype(BF16)
        fcb = ffn_conv_b[l].reshape(1, 2 * D_FF)
        final_g = final_norm_g.reshape(1, d) if l == depth - 1 else None
        xs = _ffn(xs, n2g, mx[3], mx[4], mx[5], fw_in, ffn_conv_w[l], fcb, fw_out, final_g)
        if with_ctx:
            cs = _ffn(cs, n2g, mc[3], mc[4], mc[5], fw_in, ffn_conv_w[l], fcb, fw_out)
    return xs.reshape(1, t, d)
```

```python
import functools
import math

import jax
import jax.numpy as jnp
from jax import lax
from jax.experimental import pallas as pl
from jax.experimental.pallas import tpu as pltpu

F32 = jnp.float32
BF16 = jnp.bfloat16

D_MODEL = 2048
GRID_W = 64
HEAD_DIM = 128
BLK = 128
MIX_W = D_MODEL // 2
N_Q_HEADS = MIX_W // HEAD_DIM
N_KV_HEADS = N_Q_HEADS // 4
GQA_GROUP = N_Q_HEADS // N_KV_HEADS
KV_W = N_KV_HEADS * HEAD_DIM
SGU_GROUPS = MIX_W // 128
RWKV_HEAD = 64
RWKV_PAIRS = MIX_W // (2 * RWKV_HEAD)
D_GATE_LORA = 160
D_GATE_PAD = 256
D_LORA = 64
D_FF = 5632
ROPE_THETA = 10000.0
EPS = 1e-6
GN_EPS = 64e-5
LOG2E = math.log2(math.e)
Q_SCALE = HEAD_DIM ** -0.5 * LOG2E
V_AUG = 2 * HEAD_DIM

HALO = 16
SCAN_CHUNK = 64
SCAN_GROUP_PAIRS = 8
VMEM_LIMIT = 56 * 1024 * 1024

EV_Q, EV_U, EV_Z, EV_K, EV_V = 0, 1024, 2048, 3072, 3328
EV_N = 3584
OD_Q, OD_R, OD_K, OD_V, OD_AK, OD_AV, OD_G, OD_W, OD_A = 0, 1024, 2048, 3072, 4096, 4352, 4608, 4864, 4992
OD_N = 5120


_NT = (((1,), (1,)), ((), ()))
_TN = (((0,), (0,)), ((), ()))


def _cparams(*sem):
    return pltpu.CompilerParams(dimension_semantics=sem, vmem_limit_bytes=VMEM_LIMIT)


def _sigmoid(x):
    return 1.0 / (1.0 + jnp.exp(-x))


def _gelu_tanh(x):
    return 0.5 * x * (1.0 + jnp.tanh(math.sqrt(2.0 / math.pi) * (x + 0.044715 * (x * x * x))))


def _norm_mod(xv, g, sh, sc):
    ms = jnp.mean(xv * xv, axis=-1, keepdims=True)
    y = xv * lax.rsqrt(ms + EPS) * g
    return y * (1.0 + sc) + sh


def _ada_kernel(sv_ref, w_ref, b_ref, o_ref):
    kc = 256
    nk = w_ref.shape[1] // kc
    tn = w_ref.shape[2]

    def body(c, acc):
        a0, a1 = acc
        k0 = pl.multiple_of(c * kc, kc)
        w = w_ref[0, pl.ds(k0, kc), :]
        s = sv_ref[pl.ds(k0, kc), :]
        a0 = a0 + jnp.sum(w * s[:, 0:1], axis=0, keepdims=True)
        a1 = a1 + jnp.sum(w * s[:, 1:2], axis=0, keepdims=True)
        return a0, a1

    z = jnp.zeros((1, tn), F32)
    a0, a1 = lax.fori_loop(0, nk, body, (z, z))
    b = b_ref[0]
    o_ref[0] = jnp.concatenate([a0 + b, a1 + b], axis=0)


def _ada_mod(sv, ada_w, ada_b):
    depth, d, n = ada_w.shape
    tn = 1024
    return pl.pallas_call(
        _ada_kernel,
        grid=(depth, n // tn),
        in_specs=[pl.BlockSpec((d, 2), lambda l, j: (0, 0)),
                  pl.BlockSpec((1, d, tn), lambda l, j: (l, 0, j)),
                  pl.BlockSpec((1, 1, tn), lambda l, j: (l, 0, j))],
        out_specs=pl.BlockSpec((1, 2, tn), lambda l, j: (l, 0, j)),
        out_shape=jax.ShapeDtypeStruct((depth, 2, n), F32),
        compiler_params=_cparams("parallel", "parallel"),
        name="ada_mod",
    )(sv, ada_w, ada_b.reshape(depth, 1, n))


def _fill_hn(hn_ref, x_ref, xp_ref, xn_ref, g_ref, sh_ref, sc_ref, tm, first, last):
    g, sh, sc = g_ref[...], sh_ref[...], sc_ref[...]
    rc = 128
    for r in range(tm // rc):
        hn_ref[HALO + r * rc:HALO + (r + 1) * rc, :] = _norm_mod(x_ref[r * rc:(r + 1) * rc, :], g, sh, sc).astype(BF16)
    hp = _norm_mod(xp_ref[...], g, sh, sc)
    hn_ref[0:HALO, :] = jnp.where(first, 0.0, hp).astype(BF16)
    hx = _norm_mod(xn_ref[...], g, sh, sc)
    hn_ref[HALO + tm:2 * HALO + tm, :] = jnp.where(last, 0.0, hx).astype(BF16)


def _conv3(z, cw, tm):
    rows = z.shape[0]
    zp = pltpu.roll(z, 1, 0)
    zn = pltpu.roll(z, rows - 1, 0)
    out = cw[0:1, :] * zp + cw[1:2, :] * z + cw[2:3, :] * zn
    return out[HALO:HALO + tm, :]


def _proj_kernel(x_ref, g_ref, sh_ref, sc_ref, w_ref, o_ref, hn_ref, *, tm):
    @pl.when(pl.program_id(1) == 0)
    def _():
        g, sh, sc = g_ref[...], sh_ref[...], sc_ref[...]
        rc = 128
        for r in range(tm // rc):
            hn_ref[r * rc:(r + 1) * rc, :] = _norm_mod(x_ref[r * rc:(r + 1) * rc, :], g, sh, sc).astype(BF16)

    o_ref[...] = jnp.dot(hn_ref[...], w_ref[...], preferred_element_type=F32)


def _proj_shift_kernel(x_ref, xp_ref, xn_ref, g_ref, sh_ref, sc_ref, w_ref, cw_ref, o_ref, hn_ref, *, tm,
                       plain_blocks):
    i, j = pl.program_id(0), pl.program_id(1)

    @pl.when(j == 0)
    def _():
        _fill_hn(hn_ref, x_ref, xp_ref, xn_ref, g_ref, sh_ref, sc_ref, tm, i == 0, i == pl.num_programs(0) - 1)

    z = jnp.dot(hn_ref[...], w_ref[...], preferred_element_type=F32)
    plain = functools.reduce(jnp.logical_or, [j == b for b in plain_blocks])

    @pl.when(plain)
    def _():
        o_ref[...] = z[HALO:HALO + tm, :]

    @pl.when(jnp.logical_not(plain))
    def _():
        o_ref[...] = _conv3(z, cw_ref[...], tm)


def _halo_specs(t, tm, d):
    per = tm // HALO
    nblk = t // HALO
    return [pl.BlockSpec((tm, d), lambda i, j: (i, 0)),
            pl.BlockSpec((HALO, d), lambda i, j: (jnp.maximum(i * per - 1, 0), 0)),
            pl.BlockSpec((HALO, d), lambda i, j: (jnp.minimum((i + 1) * per, nblk - 1), 0))]


def _vec_spec(d):
    return pl.BlockSpec((1, d), lambda i, j: (0, 0))


def _pick_tm(t):
    return 512 if t % 512 == 0 else 256


def _proj(x, g, sh, sc, w, conv_w=None, plain_cols=()):
    t, d = x.shape
    n = w.shape[1]
    tm = _pick_tm(t)
    tn = next(c for c in (1792, 1024, 512) if n % c == 0)
    grid = (t // tm, n // tn)
    wspec = pl.BlockSpec((d, tn), lambda i, j: (0, j))
    ospec = pl.BlockSpec((tm, tn), lambda i, j: (i, j))
    oshape = jax.ShapeDtypeStruct((t, n), F32)
    if conv_w is None:
        return pl.pallas_call(
            functools.partial(_proj_kernel, tm=tm), grid=grid,
            in_specs=[pl.BlockSpec((tm, d), lambda i, j: (i, 0)), _vec_spec(d), _vec_spec(d), _vec_spec(d), wspec],
            out_specs=ospec, out_shape=oshape,
            scratch_shapes=[pltpu.VMEM((tm, d), BF16)],
            compiler_params=_cparams("parallel", "arbitrary"), name="proj",
        )(x, g, sh, sc, w)
    plain_blocks = tuple(b for b in range(n // tn)
                         if any(lo <= b * tn and (b + 1) * tn <= hi for lo, hi in plain_cols))
    assert plain_blocks, "expected at least one identity-tap column block"
    return pl.pallas_call(
        functools.partial(_proj_shift_kernel, tm=tm, plain_blocks=plain_blocks), grid=grid,
        in_specs=_halo_specs(t, tm, d) + [_vec_spec(d), _vec_spec(d), _vec_spec(d), wspec,
                                         pl.BlockSpec((3, tn), lambda i, j: (0, j))],
        out_specs=ospec, out_shape=oshape,
        scratch_shapes=[pltpu.VMEM((tm + 2 * HALO, d), BF16)],
        compiler_params=_cparams("parallel", "arbitrary"), name="proj_shift",
    )(x, x, x, g, sh, sc, w, conv_w)


def _ffn_kernel(x_ref, xp_ref, xn_ref, g_ref, sh_ref, sc_ref, gate_ref, fg_ref, wg_ref, wv_ref, cwg_ref, cwv_ref,
                cbg_ref, cbv_ref, wo_ref, o_ref, hn_ref, acc_ref, *, tm, final_norm):
    i, j = pl.program_id(0), pl.program_id(1)

    @pl.when(j == 0)
    def _():
        _fill_hn(hn_ref, x_ref, xp_ref, xn_ref, g_ref, sh_ref, sc_ref, tm, i == 0, i == pl.num_programs(0) - 1)
        acc_ref[...] = jnp.zeros_like(acc_ref)

    hn = hn_ref[...]
    gate = _conv3(jnp.dot(hn, wg_ref[...], preferred_element_type=F32), cwg_ref[...], tm) + cbg_ref[...]
    val = _conv3(jnp.dot(hn, wv_ref[...], preferred_element_type=F32), cwv_ref[...], tm) + cbv_ref[...]
    act = (gate * _sigmoid(gate) * val).astype(BF16)
    acc_ref[...] += jnp.dot(act, wo_ref[...], preferred_element_type=F32)

    @pl.when(j == pl.num_programs(1) - 1)
    def _():
        rc = 128
        for r in range(tm // rc):
            rows = slice(r * rc, (r + 1) * rc)
            y = x_ref[rows, :] + gate_ref[...] * acc_ref[rows, :]
            if final_norm:
                y = y * lax.rsqrt(jnp.mean(y * y, axis=-1, keepdims=True) + EPS) * fg_ref[...]
            o_ref[rows, :] = y


def _ffn(x, g, sh, sc, gate, w_in, conv_w, conv_b, w_out, layer, final_g=None):
    t, d = x.shape
    f = w_out.shape[1]
    tm, tn = _pick_tm(t), 512
    nf = f // tn
    final_norm = final_g is not None
    fg = final_g if final_norm else g
    return pl.pallas_call(
        functools.partial(_ffn_kernel, tm=tm, final_norm=final_norm), grid=(t // tm, nf),
        in_specs=_halo_specs(t, tm, d) + [
            _vec_spec(d), _vec_spec(d), _vec_spec(d), _vec_spec(d), _vec_spec(d),
            pl.BlockSpec((None, d, tn), lambda i, j: (layer, 0, j)),
            pl.BlockSpec((None, d, tn), lambda i, j: (layer, 0, j + nf)),
            pl.BlockSpec((3, tn), lambda i, j: (0, j)),
            pl.BlockSpec((3, tn), lambda i, j: (0, j + nf)),
            pl.BlockSpec((1, tn), lambda i, j: (0, j)),
            pl.BlockSpec((1, tn), lambda i, j: (0, j + nf)),
            pl.BlockSpec((None, tn, d), lambda i, j: (layer, j, 0))],
        out_specs=pl.BlockSpec((tm, d), lambda i, j: (i, 0)),
        out_shape=jax.ShapeDtypeStruct((t, d), F32),
        scratch_shapes=[pltpu.VMEM((tm + 2 * HALO, d), BF16), pltpu.VMEM((tm, d), F32)],
        compiler_params=_cparams("parallel", "arbitrary"), name="conv_ffn",
    )(x, x, x, g, sh, sc, gate, fg, w_in, w_in, conv_w, conv_w, conv_b, conv_b, w_out)


def _outproj_kernel(a_ref, b_ref, x_ref, gate_ref, wa_ref, wb_ref, o_ref):
    mix = jnp.dot(a_ref[...], wa_ref[...], preferred_element_type=F32)
    mix = mix + jnp.dot(b_ref[...], wb_ref[...], preferred_element_type=F32)
    o_ref[...] = x_ref[...] + gate_ref[...] * mix


def _outproj(a, b, x, gate, wa, wb):
    t, d = x.shape
    tm = _pick_tm(t)
    return pl.pallas_call(
        _outproj_kernel, grid=(t // tm,),
        in_specs=[pl.BlockSpec((tm, MIX_W), lambda i: (i, 0)),
                  pl.BlockSpec((tm, MIX_W), lambda i: (i, 0)),
                  pl.BlockSpec((tm, d), lambda i: (i, 0)),
                  pl.BlockSpec((1, d), lambda i: (0, 0)),
                  pl.BlockSpec((MIX_W, d), lambda i: (0, 0)),
                  pl.BlockSpec((MIX_W, d), lambda i: (0, 0))],
        out_specs=pl.BlockSpec((tm, d), lambda i: (i, 0)),
        out_shape=jax.ShapeDtypeStruct((t, d), F32),
        compiler_params=_cparams("parallel"), name="out_proj",
    )(a, b, x, gate, wa, wb)


def _rope(x, cos, sin):
    lane = lax.broadcasted_iota(jnp.int32, x.shape, 1)
    up = pltpu.roll(x, 32, 1)
    dn = pltpu.roll(x, HEAD_DIM - 32, 1)
    rot = jnp.where((lane % 64) < 32, -dn, up)
    return x * cos + rot * sin


def _qk_kernel(q_ref, k_ref, v_ref, cos_ref, sin_ref, qg_ref, kg_ref, qo_ref, ko_ref, vo_ref, *, norm):
    cos, sin = cos_ref[...], sin_ref[...]

    def prep(xh, g, scale):
        if norm:
            xh = xh * lax.rsqrt(jnp.mean(xh * xh, axis=-1, keepdims=True) + EPS) * g
        xh = _rope(xh, cos, sin)
        return (xh * scale).astype(BF16) if scale != 1.0 else xh.astype(BF16)

    for h in range(N_Q_HEADS):
        sl = slice(h * HEAD_DIM, (h + 1) * HEAD_DIM)
        qo_ref[:, sl] = prep(q_ref[:, sl], qg_ref[...], Q_SCALE)
    for h in range(N_KV_HEADS):
        sl = slice(h * HEAD_DIM, (h + 1) * HEAD_DIM)
        ko_ref[:, sl] = prep(k_ref[:, sl], kg_ref[...], 1.0)
        vo_ref[:, h * V_AUG:h * V_AUG + HEAD_DIM] = v_ref[:, sl].astype(BF16)
        vo_ref[:, h * V_AUG + HEAD_DIM:(h + 1) * V_AUG] = jnp.ones((v_ref.shape[0], HEAD_DIM), BF16)


def _qk_prep(p, cos, sin, qg, kg, q_off, k_off, v_off, norm):
    t = p.shape[0]
    tm = 256
    return pl.pallas_call(
        functools.partial(_qk_kernel, norm=norm), grid=(t // tm,),
        in_specs=[pl.BlockSpec((tm, MIX_W), lambda i: (i, q_off // MIX_W)),
                  pl.BlockSpec((tm, KV_W), lambda i: (i, k_off // KV_W)),
                  pl.BlockSpec((tm, KV_W), lambda i: (i, v_off // KV_W)),
                  pl.BlockSpec((tm, HEAD_DIM), lambda i: (i, 0)),
                  pl.BlockSpec((tm, HEAD_DIM), lambda i: (i, 0)),
                  pl.BlockSpec((1, HEAD_DIM), lambda i: (0, 0)),
                  pl.BlockSpec((1, HEAD_DIM), lambda i: (0, 0))],
        out_specs=[pl.BlockSpec((tm, MIX_W), lambda i: (i, 0)),
                   pl.BlockSpec((tm, KV_W), lambda i: (i, 0)),
                   pl.BlockSpec((tm, N_KV_HEADS * V_AUG), lambda i: (i, 0))],
        out_shape=[jax.ShapeDtypeStruct((t, MIX_W), BF16),
                   jax.ShapeDtypeStruct((t, KV_W), BF16),
                   jax.ShapeDtypeStruct((t, N_KV_HEADS * V_AUG), BF16)],
        compiler_params=_cparams("parallel"), name="qk_prep",
    )(p, p, p, cos, sin, qg, kg)


def _flash_kernel(sink_ref, q_ref, k_ref, v_ref, o_ref, qs_ref, m_ref, acc_ref, *, tq, use_sink):
    g, ki = pl.program_id(0), pl.program_id(2)

    @pl.when(ki == 0)
    def _():
        for h in range(GQA_GROUP):
            qs_ref[h * tq:(h + 1) * tq, :] = q_ref[:, h * HEAD_DIM:(h + 1) * HEAD_DIM]
            if use_sink:
                m_ref[h * tq:(h + 1) * tq, :] = jnp.full((tq, 1), sink_ref[g * GQA_GROUP + h] * LOG2E, F32)
        acc_ref[:, :HEAD_DIM] = jnp.zeros((GQA_GROUP * tq, HEAD_DIM), F32)
        if use_sink:
            acc_ref[:, HEAD_DIM:] = jnp.ones((GQA_GROUP * tq, HEAD_DIM), F32)
        else:
            m_ref[...] = jnp.full_like(m_ref, -jnp.inf)
            acc_ref[:, HEAD_DIM:] = jnp.zeros((GQA_GROUP * tq, HEAD_DIM), F32)

    k, v = k_ref[...], v_ref[...]

    def scores(h):
        return lax.dot_general(qs_ref[h * tq:(h + 1) * tq, :], k, _NT, preferred_element_type=F32)

    def weigh(h, s):
        rows = slice(h * tq, (h + 1) * tq)
        m_prev = m_ref[rows, :]
        m_new = jnp.maximum(m_prev, jnp.max(s, axis=-1, keepdims=True))
        m_ref[rows, :] = m_new
        p = jnp.exp2(s - m_new).astype(BF16)
        return jnp.exp2(m_prev - m_new), jnp.dot(p, v, preferred_element_type=F32)

    s_next = scores(0)
    upd = []
    for h in range(GQA_GROUP):
        s_cur = s_next
        if h + 1 < GQA_GROUP:
            s_next = scores(h + 1)
        upd.append(weigh(h, s_cur))
    for h, (alpha, pv) in enumerate(upd):
        rows = slice(h * tq, (h + 1) * tq)
        acc_ref[rows, :] = alpha * acc_ref[rows, :] + pv

    @pl.when(ki == pl.num_programs(2) - 1)
    def _():
        out = acc_ref[:, :HEAD_DIM] / acc_ref[:, HEAD_DIM:]
        for h in range(GQA_GROUP):
            o_ref[:, h * HEAD_DIM:(h + 1) * HEAD_DIM] = out[h * tq:(h + 1) * tq, :].astype(BF16)


def _pick_tk(tk_total):
    for c in (1280, 1024, 512, 256):
        if tk_total % c == 0:
            return c
    return 128


def _dense_gqa(q, k, v, sink=None):
    tq_total, tk_total = q.shape[0], k.shape[0]
    tq = 512 if tq_total % 512 == 0 else 256
    tk = _pick_tk(tk_total)
    use_sink = sink is not None
    sink_arr = sink if use_sink else jnp.zeros((N_Q_HEADS,), F32)
    gw = GQA_GROUP * HEAD_DIM
    return pl.pallas_call(
        functools.partial(_flash_kernel, tq=tq, use_sink=use_sink),
        grid=(N_KV_HEADS, tq_total // tq, tk_total // tk),
        in_specs=[pl.BlockSpec(memory_space=pltpu.SMEM),
                  pl.BlockSpec((tq, gw), lambda g, qi, ki: (qi, g)),
                  pl.BlockSpec((tk, HEAD_DIM), lambda g, qi, ki: (ki, g)),
                  pl.BlockSpec((tk, V_AUG), lambda g, qi, ki: (ki, g))],
        out_specs=pl.BlockSpec((tq, gw), lambda g, qi, ki: (qi, g)),
        out_shape=jax.ShapeDtypeStruct((tq_total, MIX_W), BF16),
        scratch_shapes=[pltpu.VMEM((GQA_GROUP * tq, HEAD_DIM), BF16),
                        pltpu.VMEM((GQA_GROUP * tq, 1), F32),
                        pltpu.VMEM((GQA_GROUP * tq, V_AUG), F32)],
        compiler_params=_cparams("parallel", "parallel", "arbitrary"), name="dense_gqa",
    )(sink_arr, q, k, v)


def _window_kernel(sink_ref, q_ref, kp_ref, k0_ref, kx_ref, kc_ref, vp_ref, v0_ref, vx_ref, vc_ref, o_ref, *, tq):
    g, n = pl.program_id(0), pl.program_id(1)
    nq = pl.num_programs(1)
    band = tq + 2 * BLK
    kcat = jnp.concatenate([kp_ref[...], k0_ref[...], kx_ref[...], kc_ref[...]], axis=0)
    vcat = jnp.concatenate([vp_ref[...], v0_ref[...], vx_ref[...], vc_ref[...]], axis=0)
    shape = (tq, kcat.shape[0])
    row = lax.broadcasted_iota(jnp.int32, shape, 0)
    col = lax.broadcasted_iota(jnp.int32, shape, 1)
    rel = col - BLK - row
    valid = (jnp.abs(rel) <= BLK) & ((col >= BLK) | (n > 0)) & ((col < BLK + tq) | (n < nq - 1))
    valid = valid | (col >= band)

    def scores(h):
        return lax.dot_general(q_ref[:, h * HEAD_DIM:(h + 1) * HEAD_DIM], kcat, _NT, preferred_element_type=F32)

    s_next = scores(0)
    for h in range(GQA_GROUP):
        s = jnp.where(valid, s_next, -jnp.inf)
        if h + 1 < GQA_GROUP:
            s_next = scores(h + 1)
        sink = sink_ref[g * GQA_GROUP + h] * LOG2E
        m = jnp.maximum(jnp.max(s, axis=-1, keepdims=True), sink)
        p = jnp.exp2(s - m).astype(BF16)
        pv = jnp.dot(p, vcat, preferred_element_type=F32)
        out = pv[:, :HEAD_DIM] / (pv[:, HEAD_DIM:] + jnp.exp2(sink - m))
        o_ref[:, h * HEAD_DIM:(h + 1) * HEAD_DIM] = out.astype(BF16)


def _window_gqa(q, k, v, kc, vc, sink):
    t = q.shape[0]
    tq = 256 if t % 256 == 0 else BLK
    per = tq // BLK
    nb = t // BLK
    tc = kc.shape[0]
    gw = GQA_GROUP * HEAD_DIM
    prev = lambda g, n: (jnp.maximum(n * per - 1, 0), g)
    cur = lambda g, n: (n, g)
    nxt = lambda g, n: (jnp.minimum((n + 1) * per, nb - 1), g)
    ctx = lambda g, n: (0, g)
    kspecs = [pl.BlockSpec((BLK, HEAD_DIM), prev), pl.BlockSpec((tq, HEAD_DIM), cur),
              pl.BlockSpec((BLK, HEAD_DIM), nxt), pl.BlockSpec((tc, HEAD_DIM), ctx)]
    vspecs = [pl.BlockSpec((BLK, V_AUG), prev), pl.BlockSpec((tq, V_AUG), cur),
              pl.BlockSpec((BLK, V_AUG), nxt), pl.BlockSpec((tc, V_AUG), ctx)]
    return pl.pallas_call(
        functools.partial(_window_kernel, tq=tq), grid=(N_KV_HEADS, t // tq),
        in_specs=[pl.BlockSpec(memory_space=pltpu.SMEM), pl.BlockSpec((tq, gw), lambda g, n: (n, g))]
        + kspecs + vspecs,
        out_specs=pl.BlockSpec((tq, gw), lambda g, n: (n, g)),
        out_shape=jax.ShapeDtypeStruct((t, MIX_W), BF16),
        compiler_params=_cparams("parallel", "parallel"), name="window_gqa",
    )(sink, q, k, k, k, kc, v, v, v, vc)


def _sgu_kernel(u_ref, z_ref, ng_ref, ws_ref, bs_ref, o_ref, *, chunks):
    for c in range(chunks):
        rows = slice(c * BLK, (c + 1) * BLK)
        for g in range(SGU_GROUPS):
            cols = slice(g * 128, (g + 1) * 128)
            z = _gelu_tanh(z_ref[rows, cols])
            mu = jnp.mean(z, axis=-1, keepdims=True)
            zc = z - mu
            var = jnp.mean(zc * zc, axis=-1, keepdims=True)
            vn = zc * lax.rsqrt(var + EPS) * ng_ref[:, cols]
            vm = jnp.dot(ws_ref[g], vn.astype(BF16), preferred_element_type=F32) + bs_ref[g]
            o_ref[rows, cols] = (_gelu_tanh(u_ref[rows, cols]) * vm).astype(BF16)


def _sgu(p, norm_g, w_s, b_s):
    t = p.shape[0]
    chunks = 2
    tm = chunks * BLK
    bs = jnp.broadcast_to(b_s[:, :, None], (SGU_GROUPS, BLK, 128))
    return pl.pallas_call(
        functools.partial(_sgu_kernel, chunks=chunks), grid=(t // tm,),
        in_specs=[pl.BlockSpec((tm, MIX_W), lambda i: (i, EV_U // MIX_W)),
                  pl.BlockSpec((tm, MIX_W), lambda i: (i, EV_Z // MIX_W)),
                  pl.BlockSpec((1, MIX_W), lambda i: (0, 0)),
                  pl.BlockSpec((SGU_GROUPS, BLK, BLK), lambda i: (0, 0, 0)),
                  pl.BlockSpec((SGU_GROUPS, BLK, 128), lambda i: (0, 0, 0))],
        out_specs=pl.BlockSpec((tm, MIX_W), lambda i: (i, 0)),
        out_shape=jax.ShapeDtypeStruct((t, MIX_W), BF16),
        compiler_params=_cparams("parallel"), name="chunk_sgu",
    )(p, p, norm_g.reshape(1, MIX_W), w_s.astype(BF16), bs)


def _split3(x):
    hi = x.astype(BF16)
    r1 = x - hi.astype(F32)
    mid = r1.astype(BF16)
    lo = (r1 - mid.astype(F32)).astype(BF16)
    return hi, mid, lo


def _seg_sum(x, e):
    parts = []
    for b in range(x.shape[1] // 128):
        hi, mid, lo = _split3(x[:, b * 128:(b + 1) * 128])
        parts.append(jnp.dot(hi, e, preferred_element_type=F32) + jnp.dot(mid, e, preferred_element_type=F32)
                     + jnp.dot(lo, e, preferred_element_type=F32))
    return jnp.concatenate(parts, axis=1)


def _run_sum(tri, x):
    hi, mid, lo = _split3(x)
    return (jnp.dot(tri, hi, preferred_element_type=F32) + jnp.dot(tri, mid, preferred_element_type=F32)
            + jnp.dot(tri, lo, preferred_element_type=F32))


def _feat_kernel(r_ref, k_ref, v_ref, gl_ref, wl_ref, al_ref, g2_ref, w2_ref, a2_ref, w0_ref, a0_ref,
                 kk_ref, ka_ref, rk_ref, e_ref, tri_ref, ld_ref, cum_ref, kd_ref, ad_ref, kko_ref, bon_ref, go_ref):
    e = e_ref[...]
    r, k, v = r_ref[...], k_ref[...], v_ref[...]
    go_ref[...] = jnp.dot(_sigmoid(gl_ref[...]), g2_ref[...], preferred_element_type=F32)
    lw = w0_ref[...] + jnp.dot(jnp.tanh(wl_ref[...]), w2_ref[...], preferred_element_type=F32)
    nl = -lw
    softplus = jnp.maximum(nl, 0.0) + jnp.log(1.0 + jnp.exp(-jnp.abs(nl)))
    ld = -jnp.exp(-softplus - 0.5)
    ld_ref[...] = ld
    cum_ref[:, :MIX_W] = _run_sum(tri_ref[0], ld[:, :MIX_W])
    cum_ref[:, MIX_W:] = _run_sum(tri_ref[1], ld[:, MIX_W:])
    a = _sigmoid(a0_ref[...] + jnp.dot(al_ref[...], a2_ref[...], preferred_element_type=F32))
    ad_ref[...] = a
    kk = k * kk_ref[...]
    nrm = jnp.maximum(jnp.sqrt(_seg_sum(kk * kk, e)), 1e-12)
    kko_ref[...] = kk / nrm
    ka = ka_ref[...]
    kd0 = k * (1.0 + (a[:, :MIX_W] - 1.0) * ka)
    kd1 = k * (1.0 + (a[:, MIX_W:] - 1.0) * ka)
    kd_ref[:, :MIX_W] = kd0
    kd_ref[:, MIX_W:] = kd1
    bon_ref[...] = _seg_sum(r * (kd0 + kd1) * rk_ref[...], e) * v


def _rwkv_features(zs, g2p, w2b, a2b, w0, a0, k_k, k_a, r_k, e):
    t = zs.shape[0]
    tm = 256
    col = lambda w, off: pl.BlockSpec((tm, w), lambda i: (i, off // w))
    full = lambda a: pl.BlockSpec(a.shape, lambda i: (0,) * a.ndim)
    wide = lambda w: pl.BlockSpec((tm, w), lambda i: (i, 0))
    row = jnp.arange(tm)
    same_chunk = (row[:, None] // SCAN_CHUNK) == (row[None, :] // SCAN_CHUNK)
    tri = jnp.stack([same_chunk & (row[None, :] <= row[:, None]),
                     same_chunk & (row[None, :] >= row[:, None])]).astype(BF16)
    params = [g2p, w2b, a2b, w0, a0, k_k, k_a, r_k, e, tri]
    return pl.pallas_call(
        _feat_kernel, grid=(t // tm,),
        in_specs=[col(MIX_W, OD_R), col(MIX_W, OD_K), col(MIX_W, OD_V), col(D_GATE_PAD, OD_G),
                  col(128, OD_W), col(128, OD_A)] + [full(a) for a in params],
        out_specs=[wide(2 * MIX_W)] * 4 + [wide(MIX_W)] * 3,
        out_shape=[jax.ShapeDtypeStruct((t, 2 * MIX_W), F32)] * 4 + [jax.ShapeDtypeStruct((t, MIX_W), F32)] * 3,
        compiler_params=_cparams("parallel"), name="rwkv_features",
    )(zs, zs, zs, zs, zs, zs, *params)


def _bdot(a, b, dims=(((1,), (0,)), ((), ()))):
    return lax.dot_general(a.astype(BF16), b.astype(BF16), dims, preferred_element_type=F32)


def _scan_masks():
    L = SCAN_CHUNK
    r = jnp.arange(2 * L)[:, None]
    c = jnp.arange(2 * L)[None, :]
    same = (r // L) == (c // L)
    out = []
    for reverse in (False, True):
        before = (c > r) if reverse else (c < r)
        out += [same & before, same & (before | (c == r))]
        b = 1
        while b < L:
            late_r, late_c = (r // b) % 2 == 1, (c // b) % 2 == 1
            couple = (~late_r & late_c) if reverse else (late_r & ~late_c)
            out.append(((r // (2 * b)) == (c // (2 * b))) & couple)
            b *= 2
    return jnp.stack(out).astype(F32), same.astype(F32)


def _chunk_group(streams, mask_ref, nlv, same, eye, m0, m1):
    L = SCAN_CHUNK
    n = len(streams)
    pre = []
    for ld, cum, kd, ad, r, v, kk, s_prev, d in streams:
        g_inc = jnp.exp(cum)
        g_exc = jnp.exp(cum - ld)
        g_inv = jnp.exp(-cum)
        last = 0 if d else L - 1
        g_tot = g_inc[last:last + 1, :]
        at = -kk * g_exc
        rt = r * g_inc
        bt = (kk * ad * g_inv).astype(BF16)
        kt = (kd * g_inv).astype(BF16)
        lar = jnp.concatenate([at * m0, at * m1, rt * m0, rt * m1], axis=0).astype(BF16)
        vbd = jnp.concatenate([v * m0, v * m1], axis=0).astype(BF16)
        pre.append((lar, bt, kt, vbd, v.astype(BF16), g_tot, s_prev, d * nlv))
    gram = [_bdot(lar, jnp.concatenate([bt, bt, kt, kt], axis=0), _NT) for lar, bt, kt, _, _, _, _, _ in pre]
    xs = [_bdot(p[0], p[6], _NT) for p in pre]
    n_ab, m_ak, m_r = [], [], []
    for g, p in zip(gram, pre):
        strict, incl = mask_ref[p[7]] > 0, mask_ref[p[7] + 1] > 0
        n_ab.append(jnp.where(strict, g[:2 * L, :2 * L], 0.0))
        m_ak.append(jnp.where(strict, g[:2 * L, 2 * L:], 0.0).astype(BF16))
        m_r.append(jnp.concatenate([jnp.where(incl, g[2 * L:, :2 * L], 0.0),
                                    jnp.where(incl, g[2 * L:, 2 * L:], 0.0)], axis=1).astype(BF16))
    x = [xs[i][:2 * L] + _bdot(m_ak[i], pre[i][3]) for i in range(n)]
    tinv = [eye + n_ab[i] * mask_ref[pre[i][7] + 2] for i in range(n)]
    for q in range(3, nlv):
        tb = [t.astype(BF16) for t in tinv]
        half = [_bdot(tb[i], n_ab[i] * mask_ref[pre[i][7] + q]) for i in range(n)]
        tinv = [tinv[i] + _bdot(half[i], tb[i]) for i in range(n)]
    ubd = [_bdot(tinv[i], x[i]) for i in range(n)]
    ybd = [xs[i][2 * L:] + _bdot(m_r[i], jnp.concatenate([ubd[i].astype(BF16), pre[i][3]], axis=0)) for i in range(n)]
    out = []
    for i in range(n):
        _, bt, kt, _, vb, g_tot, s_prev, _ = pre[i]
        u = (ubd[i][:L, :] + ubd[i][L:, :]).astype(BF16)
        upd = _bdot(jnp.concatenate([u, vb], axis=0), jnp.concatenate([bt, kt], axis=0), _TN)
        out.append((ybd[i][:L, :] + ybd[i][L:, :], (s_prev + upd * same) * g_tot))
    return out


def _scan_kernel(ldf_ref, cumf_ref, kdf_ref, adf_ref, rf_ref, vf_ref, kkf_ref,
                 ldb_ref, cumb_ref, kdb_ref, adb_ref, rb_ref, vb_ref, kkb_ref,
                 s0f_ref, s0b_ref, mask_ref, same_ref, eye_ref,
                 yf_ref, yb_ref, sff_ref, sfb_ref, sf_ref, sb_ref):
    L = SCAN_CHUNK
    c = pl.program_id(0)

    @pl.when(c == 0)
    def _():
        sf_ref[...] = s0f_ref[...]
        sb_ref[...] = s0b_ref[...]

    lane = lax.broadcasted_iota(jnp.int32, (L, 2 * RWKV_HEAD), 1)
    m0 = (lane < RWKV_HEAD).astype(F32)
    m1 = 1.0 - m0
    nlv = mask_ref.shape[0] // 2
    same, eye = same_ref[...], eye_ref[...]
    dirs = ((ldf_ref, cumf_ref, kdf_ref, adf_ref, rf_ref, vf_ref, kkf_ref, sf_ref, yf_ref, 0),
            (ldb_ref, cumb_ref, kdb_ref, adb_ref, rb_ref, vb_ref, kkb_ref, sb_ref, yb_ref, 1))
    for p0 in range(0, RWKV_PAIRS, SCAN_GROUP_PAIRS):
        streams, dests = [], []
        for p in range(p0, p0 + SCAN_GROUP_PAIRS):
            cols = slice(p * 128, (p + 1) * 128)
            for ld_ref, cum_ref, kd_ref, ad_ref, r_ref, v_ref, kk_ref, s_ref, y_ref, d in dirs:
                streams.append((ld_ref[:, cols], cum_ref[:, cols], kd_ref[:, cols], ad_ref[:, cols],
                                r_ref[:, cols], v_ref[:, cols], kk_ref[:, cols], s_ref[p], d))
                dests.append((y_ref, s_ref, p, cols))
        for (y_ref, s_ref, p, cols), (y, s_new) in zip(dests, _chunk_group(streams, mask_ref, nlv, same, eye, m0, m1)):
            y_ref[:, cols] = y
            s_ref[p] = s_new

    @pl.when(c == pl.num_programs(0) - 1)
    def _():
        sff_ref[...] = sf_ref[...]
        sfb_ref[...] = sb_ref[...]


def _rwkv_scan(ld, cum, kd, ad, zs, kk, s0f, s0b):
    t = zs.shape[0]
    L = SCAN_CHUNK
    nc = t // L
    masks, same = _scan_masks()
    eye = jnp.eye(2 * L, dtype=F32)
    specs = []
    for d, cidx in ((0, lambda c: c), (1, lambda c: nc - 1 - c)):
        dirblk = pl.BlockSpec((L, MIX_W), lambda c, cidx=cidx, d=d: (cidx(c), d))
        specs += [dirblk, dirblk, dirblk, dirblk,
                  pl.BlockSpec((L, MIX_W), lambda c, cidx=cidx: (cidx(c), OD_R // MIX_W)),
                  pl.BlockSpec((L, MIX_W), lambda c, cidx=cidx: (cidx(c), OD_V // MIX_W)),
                  pl.BlockSpec((L, MIX_W), lambda c, cidx=cidx: (cidx(c), 0))]
    sblk = pl.BlockSpec((RWKV_PAIRS, 128, 128), lambda c: (0, 0, 0))
    const = lambda a: pl.BlockSpec(a.shape, lambda c: (0,) * a.ndim)
    feats = (ld, cum, kd, ad, zs, zs, kk)
    return pl.pallas_call(
        _scan_kernel, grid=(nc,),
        in_specs=specs + [sblk, sblk, const(masks), const(same), const(eye)],
        out_specs=[pl.BlockSpec((L, MIX_W), lambda c: (c, 0)),
                   pl.BlockSpec((L, MIX_W), lambda c: (nc - 1 - c, 0)), sblk, sblk],
        out_shape=[jax.ShapeDtypeStruct((t, MIX_W), F32)] * 2 + [jax.ShapeDtypeStruct((RWKV_PAIRS, 128, 128), F32)] * 2,
        scratch_shapes=[pltpu.VMEM((RWKV_PAIRS, 128, 128), F32)] * 2,
        compiler_params=_cparams("arbitrary"), name="rwkv_scan",
    )(*feats, *feats, s0f, s0b, masks, same, eye)


def _readout_kernel(yf_ref, yb_ref, bon_ref, g_ref, lw_ref, lb_ref, e_ref, o_ref):
    e = e_ref[...]
    y = yf_ref[...] + yb_ref[...]
    mu = _seg_sum(y, e) * (1.0 / RWKV_HEAD)
    yc = y - mu
    var = _seg_sum(yc * yc, e) * (1.0 / RWKV_HEAD)
    yn = yc * lax.rsqrt(var + GN_EPS) * lw_ref[...] + lb_ref[...]
    o_ref[...] = ((yn + bon_ref[...]) * g_ref[...]).astype(BF16)


def _rwkv_readout(yf, yb, bonus, g, ln_w, ln_b, e):
    t = yf.shape[0]
    tm = 256
    wide = pl.BlockSpec((tm, MIX_W), lambda i: (i, 0))
    vec = pl.BlockSpec((1, MIX_W), lambda i: (0, 0))
    return pl.pallas_call(
        _readout_kernel, grid=(t // tm,),
        in_specs=[wide, wide, wide, wide, vec, vec, pl.BlockSpec((128, 128), lambda i: (0, 0))],
        out_specs=wide, out_shape=jax.ShapeDtypeStruct((t, MIX_W), BF16),
        compiler_params=_cparams("parallel"), name="rwkv_readout",
    )(yf, yb, bonus, g, ln_w, ln_b, e)


def _rope_tables(rows):
    half = HEAD_DIM // 2
    inv = ROPE_THETA ** (-jnp.arange(0, half, 2, dtype=F32) / half)
    row = jnp.repeat(jnp.arange(rows, dtype=F32), GRID_W)
    col = jnp.tile(jnp.arange(GRID_W, dtype=F32), rows)
    ang_r = row[:, None] * inv[None, :]
    ang_c = col[:, None] * inv[None, :]
    ang = jnp.concatenate([ang_r, ang_r, ang_c, ang_c], axis=-1)
    return jnp.cos(ang), jnp.sin(ang)


def _even_layer(x, ctx, mx, mc, n1g, w_in, w_out, qg, kg, vng, w_s, b_s, tabs, with_ctx):
    o_k = N_Q_HEADS * HEAD_DIM
    o_v = o_k + KV_W
    o_u = o_v + KV_W
    o_z = o_u + MIX_W
    wp = jnp.concatenate([w_in[:, :o_k], w_in[:, o_u:o_z], w_in[:, o_z:], w_in[:, o_k:o_v], w_in[:, o_v:o_u]],
                         axis=1).astype(BF16)
    wa, wb = w_out[:MIX_W].astype(BF16), w_out[MIX_W:].astype(BF16)
    qg, kg = qg.reshape(1, HEAD_DIM), kg.reshape(1, HEAD_DIM)
    (cos_x, sin_x), (cos_c, sin_c) = tabs
    px = _proj(x, n1g, mx[0], mx[1], wp)
    pc = _proj(ctx, n1g, mc[0], mc[1], wp)
    qx, kx, vx = _qk_prep(px, cos_x, sin_x, qg, kg, EV_Q, EV_K, EV_V, True)
    qc, kc, vc = _qk_prep(pc, cos_c, sin_c, qg, kg, EV_Q, EV_K, EV_V, True)
    ax = _dense_gqa(qx, jnp.concatenate([kx, kc], axis=0), jnp.concatenate([vx, vc], axis=0))
    bx = _sgu(px, vng, w_s, b_s)
    x = _outproj(ax, bx, x, mx[2], wa, wb)
    if with_ctx:
        ac = _dense_gqa(qc, kc, vc)
        bc = _sgu(pc, vng, w_s, b_s)
        ctx = _outproj(ac, bc, ctx, mc[2], wa, wb)
    return x, ctx


def _odd_layer(x, ctx, mx, mc, n1g, w_in, w_out, sink, shift_mu, w0, w2, a0, a2, g2, k_k, k_a, r_k, ln_w, ln_b,
               tabs, with_ctx):
    o_k = N_Q_HEADS * HEAD_DIM
    o_v = o_k + KV_W
    c_in = o_v + KV_W
    o_g = 3 * MIX_W
    o_w = o_g + D_GATE_LORA
    o_a = o_w + 2 * D_LORA
    d = w_in.shape[0]
    wr = w_in[:, c_in:]
    gpad = jnp.zeros((d, D_GATE_PAD - D_GATE_LORA), w_in.dtype)
    wp = jnp.concatenate([w_in[:, :o_k], wr[:, :o_g], w_in[:, o_k:o_v], w_in[:, o_v:c_in],
                          wr[:, o_g:o_w], gpad, wr[:, o_w:o_a], wr[:, o_a:]], axis=1).astype(BF16)
    taps = jnp.stack([shift_mu[0], 1.0 - shift_mu[0] - shift_mu[1], shift_mu[1]], axis=0)
    ident = jnp.tile(jnp.array([[0.0], [1.0], [0.0]], F32), (1, 1))
    cw = jnp.concatenate([jnp.tile(ident, (1, MIX_W)), taps[:, :o_g], jnp.tile(ident, (1, 2 * KV_W)),
                          taps[:, o_g:o_w], jnp.tile(ident, (1, D_GATE_PAD - D_GATE_LORA)),
                          taps[:, o_w:o_a], taps[:, o_a:]], axis=1)
    wa, wb = w_out[:MIX_W].astype(BF16), w_out[MIX_W:].astype(BF16)
    g2p = jnp.concatenate([g2, jnp.zeros((D_GATE_PAD - D_GATE_LORA, MIX_W), g2.dtype)], axis=0)
    zl = jnp.zeros((D_LORA, MIX_W), F32)
    w2b = jnp.concatenate([jnp.concatenate([w2[0], zl], axis=1), jnp.concatenate([zl, w2[1]], axis=1)], axis=0)
    a2b = jnp.concatenate([jnp.concatenate([a2[0], zl], axis=1), jnp.concatenate([zl, a2[1]], axis=1)], axis=0)
    w0f, a0f = w0.reshape(1, 2 * MIX_W), a0.reshape(1, 2 * MIX_W)
    kkv, kav, rkv = k_k.reshape(1, MIX_W), k_a.reshape(1, MIX_W), r_k.reshape(1, MIX_W)
    lnw, lnb = ln_w.reshape(1, MIX_W), ln_b.reshape(1, MIX_W)
    lane = jnp.arange(128)
    e = (lane[:, None] // RWKV_HEAD == lane[None, :] // RWKV_HEAD).astype(BF16)
    ones_g = jnp.ones((1, HEAD_DIM), F32)
    (cos_x, sin_x), (cos_c, sin_c) = tabs

    plain = ((OD_Q, OD_R), (OD_AK, OD_G))
    px = _proj(x, n1g, mx[0], mx[1], wp, cw, plain)
    pc = _proj(ctx, n1g, mc[0], mc[1], wp, cw, plain)
    qx, kx, vx = _qk_prep(px, cos_x, sin_x, ones_g, ones_g, OD_Q, OD_AK, OD_AV, False)
    qc, kc, vc = _qk_prep(pc, cos_c, sin_c, ones_g, ones_g, OD_Q, OD_AK, OD_AV, False)
    cx = _window_gqa(qx, kx, vx, kc, vc, sink)

    fparams = (g2p, w2b, a2b, w0f, a0f, kkv, kav, rkv, e)
    ld_x, cum_x, kd_x, ad_x, kk_x, bon_x, g_x = _rwkv_features(px, *fparams)
    ld_c, cum_c, kd_c, ad_c, kk_c, bon_c, g_c = _rwkv_features(pc, *fparams)
    s0 = jnp.zeros((RWKV_PAIRS, 128, 128), F32)
    y_cf, y_cb, s_cf, s_cb = _rwkv_scan(ld_c, cum_c, kd_c, ad_c, pc, kk_c, s0, s0)
    y_xf, y_xb, _, _ = _rwkv_scan(ld_x, cum_x, kd_x, ad_x, px, kk_x, s_cf, s_cb)
    dx = _rwkv_readout(y_xf, y_xb, bon_x, g_x, lnw, lnb, e)
    x = _outproj(cx, dx, x, mx[2], wa, wb)
    if with_ctx:
        cc = _dense_gqa(qc, kc, vc, sink)
        dc = _rwkv_readout(y_cf, y_cb, bon_c, g_c, lnw, lnb, e)
        ctx = _outproj(cc, dc, ctx, mc[2], wa, wb)
    return x, ctx


def kernel(x, c, ctx, c_ctx, ada_w, ada_b, norm1_g, norm2_g, ffn_w_in, ffn_conv_w, ffn_conv_b, ffn_w_out,
           ev_w_in, ev_w_out, a_q_norm_g, a_k_norm_g, b_v_norm_g, b_spatial_w, b_spatial_b,
           od_w_in, od_w_out, c_sink, d_shift_mu, d_w0, d_w2, d_a0, d_a2, d_g2, d_k_k, d_k_a, d_r_k,
           d_ln_w, d_ln_b, final_norm_g):
    bsz, t, d = x.shape
    assert bsz == 1 and d == D_MODEL and ada_w.shape[0] >= 1
    tc = ctx.shape[1]
    depth = ada_w.shape[0]
    xs, cs = x.reshape(t, d), ctx.reshape(tc, d)
    tabs = (_rope_tables(t // GRID_W), (jnp.ones((tc, HEAD_DIM), F32), jnp.zeros((tc, HEAD_DIM), F32)))
    sv = jnp.stack([jax.nn.silu(c[0]), jax.nn.silu(c_ctx)], axis=1)
    mods = _ada_mod(sv, ada_w, ada_b)
    fw_in, fw_out = ffn_w_in.astype(BF16), ffn_w_out.astype(BF16)
    for l in range(depth):
        with_ctx = l < depth - 1
        mx = [mods[l, 0, k * d:(k + 1) * d].reshape(1, d) for k in range(6)]
        mc = [mods[l, 1, k * d:(k + 1) * d].reshape(1, d) for k in range(6)]
        n1g, n2g = norm1_g[l].reshape(1, d), norm2_g[l].reshape(1, d)
        i = l // 2
        if l % 2 == 0:
            xs, cs = _even_layer(xs, cs, mx, mc, n1g, ev_w_in[i], ev_w_out[i], a_q_norm_g[i], a_k_norm_g[i],
                                 b_v_norm_g[i], b_spatial_w[i], b_spatial_b[i], tabs, with_ctx)
        else:
            xs, cs = _odd_layer(xs, cs, mx, mc, n1g, od_w_in[i], od_w_out[i], c_sink[i], d_shift_mu[i], d_w0[i],
                                d_w2[i], d_a0[i], d_a2[i], d_g2[i], d_k_k[i], d_k_a[i], d_r_k[i], d_ln_w[i],
                                d_ln_b[i], tabs, with_ctx)
        fcb = ffn_conv_b[l].reshape(1, 2 * D_FF)
        final_g = final_norm_g.reshape(1, d) if l == depth - 1 else None
        xs = _ffn(xs, n2g, mx[3], mx[4], mx[5], fw_in, ffn_conv_w[l], fcb, fw_out, l, final_g)
        if with_ctx:
            cs = _ffn(cs, n2g, mc[3], mc[4], mc[5], fw_in, ffn_conv_w[l], fcb, fw_out, l)
    return xs.reshape(1, t, d)
```

```python
import functools
import math

import jax
import jax.numpy as jnp
from jax import lax
from jax.experimental import pallas as pl
from jax.experimental.pallas import tpu as pltpu

F32 = jnp.float32
BF16 = jnp.bfloat16

D_MODEL = 2048
GRID_W = 64
HEAD_DIM = 128
BLK = 128
MIX_W = D_MODEL // 2
N_Q_HEADS = MIX_W // HEAD_DIM
N_KV_HEADS = N_Q_HEADS // 4
GQA_GROUP = N_Q_HEADS // N_KV_HEADS
KV_W = N_KV_HEADS * HEAD_DIM
SGU_GROUPS = MIX_W // 128
RWKV_HEAD = 64
RWKV_PAIRS = MIX_W // (2 * RWKV_HEAD)
D_GATE_LORA = 160
D_GATE_PAD = 256
D_LORA = 64
D_FF = 5632
ROPE_THETA = 10000.0
EPS = 1e-6
GN_EPS = 64e-5
LOG2E = math.log2(math.e)
Q_SCALE = HEAD_DIM ** -0.5 * LOG2E
V_AUG = 2 * HEAD_DIM

HALO = 16
SCAN_CHUNK = 64
SCAN_GROUP_PAIRS = 8
VMEM_LIMIT = 56 * 1024 * 1024

EV_Q, EV_U, EV_Z, EV_K, EV_V = 0, 1024, 2048, 3072, 3328
EV_N = 3584
OD_Q, OD_R, OD_K, OD_V, OD_AK, OD_AV, OD_G, OD_W, OD_A = 0, 1024, 2048, 3072, 4096, 4352, 4608, 4864, 4992
OD_N = 5120


_NT = (((1,), (1,)), ((), ()))
_TN = (((0,), (0,)), ((), ()))


def _cparams(*sem):
    return pltpu.CompilerParams(dimension_semantics=sem, vmem_limit_bytes=VMEM_LIMIT)


def _sigmoid(x):
    return 1.0 / (1.0 + jnp.exp(-x))


def _gelu_tanh(x):
    return 0.5 * x * (1.0 + jnp.tanh(math.sqrt(2.0 / math.pi) * (x + 0.044715 * (x * x * x))))


def _norm_mod(xv, g, sh, sc):
    ms = jnp.mean(xv * xv, axis=-1, keepdims=True)
    y = xv * lax.rsqrt(ms + EPS) * g
    return y * (1.0 + sc) + sh


def _ada_kernel(sv_ref, w_ref, b_ref, o_ref):
    kc = 256
    nk = w_ref.shape[1] // kc
    tn = w_ref.shape[2]

    def body(c, acc):
        a0, a1 = acc
        k0 = pl.multiple_of(c * kc, kc)
        w = w_ref[0, pl.ds(k0, kc), :]
        s = sv_ref[pl.ds(k0, kc), :]
        a0 = a0 + jnp.sum(w * s[:, 0:1], axis=0, keepdims=True)
        a1 = a1 + jnp.sum(w * s[:, 1:2], axis=0, keepdims=True)
        return a0, a1

    z = jnp.zeros((1, tn), F32)
    a0, a1 = lax.fori_loop(0, nk, body, (z, z))
    b = b_ref[0]
    o_ref[0] = jnp.concatenate([a0 + b, a1 + b], axis=0)


def _ada_mod(sv, ada_w, ada_b):
    depth, d, n = ada_w.shape
    tn = 1024
    return pl.pallas_call(
        _ada_kernel,
        grid=(depth, n // tn),
        in_specs=[pl.BlockSpec((d, 2), lambda l, j: (0, 0)),
                  pl.BlockSpec((1, d, tn), lambda l, j: (l, 0, j)),
                  pl.BlockSpec((1, 1, tn), lambda l, j: (l, 0, j))],
        out_specs=pl.BlockSpec((1, 2, tn), lambda l, j: (l, 0, j)),
        out_shape=jax.ShapeDtypeStruct((depth, 2, n), F32),
        compiler_params=_cparams("parallel", "parallel"),
        name="ada_mod",
    )(sv, ada_w, ada_b.reshape(depth, 1, n))


def _fill_hn(hn_ref, x_ref, xp_ref, xn_ref, g_ref, sh_ref, sc_ref, tm, first, last):
    g, sh, sc = g_ref[...], sh_ref[...], sc_ref[...]
    rc = 128
    for r in range(tm // rc):
        hn_ref[HALO + r * rc:HALO + (r + 1) * rc, :] = _norm_mod(x_ref[r * rc:(r + 1) * rc, :], g, sh, sc).astype(BF16)
    hp = _norm_mod(xp_ref[...], g, sh, sc)
    hn_ref[0:HALO, :] = jnp.where(first, 0.0, hp).astype(BF16)
    hx = _norm_mod(xn_ref[...], g, sh, sc)
    hn_ref[HALO + tm:2 * HALO + tm, :] = jnp.where(last, 0.0, hx).astype(BF16)


def _conv3(z, cw, tm):
    rows = z.shape[0]
    zp = pltpu.roll(z, 1, 0)
    zn = pltpu.roll(z, rows - 1, 0)
    out = cw[0:1, :] * zp + cw[1:2, :] * z + cw[2:3, :] * zn
    return out[HALO:HALO + tm, :]


def _proj_kernel(x_ref, g_ref, sh_ref, sc_ref, w_ref, o_ref, hn_ref, *, tm):
    @pl.when(pl.program_id(1) == 0)
    def _():
        g, sh, sc = g_ref[...], sh_ref[...], sc_ref[...]
        rc = 128
        for r in range(tm // rc):
            hn_ref[r * rc:(r + 1) * rc, :] = _norm_mod(x_ref[r * rc:(r + 1) * rc, :], g, sh, sc).astype(BF16)

    o_ref[...] = jnp.dot(hn_ref[...], w_ref[...], preferred_element_type=F32)


def _proj_shift_kernel(x_ref, xp_ref, xn_ref, g_ref, sh_ref, sc_ref, w_ref, cw_ref, o_ref, hn_ref, *, tm,
                       plain_blocks):
    i, j = pl.program_id(0), pl.program_id(1)

    @pl.when(j == 0)
    def _():
        _fill_hn(hn_ref, x_ref, xp_ref, xn_ref, g_ref, sh_ref, sc_ref, tm, i == 0, i == pl.num_programs(0) - 1)

    z = jnp.dot(hn_ref[...], w_ref[...], preferred_element_type=F32)
    plain = functools.reduce(jnp.logical_or, [j == b for b in plain_blocks])

    @pl.when(plain)
    def _():
        o_ref[...] = z[HALO:HALO + tm, :]

    @pl.when(jnp.logical_not(plain))
    def _():
        o_ref[...] = _conv3(z, cw_ref[...], tm)


def _halo_specs(t, tm, d):
    per = tm // HALO
    nblk = t // HALO
    return [pl.BlockSpec((tm, d), lambda i, j: (i, 0)),
            pl.BlockSpec((HALO, d), lambda i, j: (jnp.maximum(i * per - 1, 0), 0)),
            pl.BlockSpec((HALO, d), lambda i, j: (jnp.minimum((i + 1) * per, nblk - 1), 0))]


def _vec_spec(d):
    return pl.BlockSpec((1, d), lambda i, j: (0, 0))


def _pick_tm(t):
    return 512 if t % 512 == 0 else 256


def _proj(x, g, sh, sc, w, conv_w=None, plain_cols=()):
    t, d = x.shape
    n = w.shape[1]
    tm = _pick_tm(t)
    tn = next(c for c in (1792, 1024, 512) if n % c == 0)
    grid = (t // tm, n // tn)
    wspec = pl.BlockSpec((d, tn), lambda i, j: (0, j))
    ospec = pl.BlockSpec((tm, tn), lambda i, j: (i, j))
    oshape = jax.ShapeDtypeStruct((t, n), F32)
    if conv_w is None:
        return pl.pallas_call(
            functools.partial(_proj_kernel, tm=tm), grid=grid,
            in_specs=[pl.BlockSpec((tm, d), lambda i, j: (i, 0)), _vec_spec(d), _vec_spec(d), _vec_spec(d), wspec],
            out_specs=ospec, out_shape=oshape,
            scratch_shapes=[pltpu.VMEM((tm, d), BF16)],
            compiler_params=_cparams("parallel", "arbitrary"), name="proj",
        )(x, g, sh, sc, w)
    plain_blocks = tuple(b for b in range(n // tn)
                         if any(lo <= b * tn and (b + 1) * tn <= hi for lo, hi in plain_cols))
    assert plain_blocks, "expected at least one identity-tap column block"
    return pl.pallas_call(
        functools.partial(_proj_shift_kernel, tm=tm, plain_blocks=plain_blocks), grid=grid,
        in_specs=_halo_specs(t, tm, d) + [_vec_spec(d), _vec_spec(d), _vec_spec(d), wspec,
                                         pl.BlockSpec((3, tn), lambda i, j: (0, j))],
        out_specs=ospec, out_shape=oshape,
        scratch_shapes=[pltpu.VMEM((tm + 2 * HALO, d), BF16)],
        compiler_params=_cparams("parallel", "arbitrary"), name="proj_shift",
    )(x, x, x, g, sh, sc, w, conv_w)


def _ffn_kernel(x_ref, xp_ref, xn_ref, g_ref, sh_ref, sc_ref, gate_ref, fg_ref, wg_ref, wv_ref, cwg_ref, cwv_ref,
                cbg_ref, cbv_ref, wo_ref, o_ref, hn_ref, acc_ref, *, tm, final_norm):
    i, j = pl.program_id(0), pl.program_id(1)

    @pl.when(j == 0)
    def _():
        _fill_hn(hn_ref, x_ref, xp_ref, xn_ref, g_ref, sh_ref, sc_ref, tm, i == 0, i == pl.num_programs(0) - 1)
        acc_ref[...] = jnp.zeros_like(acc_ref)

    hn = hn_ref[...]
    gate = _conv3(jnp.dot(hn, wg_ref[...], preferred_element_type=F32), cwg_ref[...], tm) + cbg_ref[...]
    val = _conv3(jnp.dot(hn, wv_ref[...], preferred_element_type=F32), cwv_ref[...], tm) + cbv_ref[...]
    act = (gate * _sigmoid(gate) * val).astype(BF16)
    acc_ref[...] += jnp.dot(act, wo_ref[...], preferred_element_type=F32)

    @pl.when(j == pl.num_programs(1) - 1)
    def _():
        rc = 128
        for r in range(tm // rc):
            rows = slice(r * rc, (r + 1) * rc)
            y = x_ref[rows, :] + gate_ref[...] * acc_ref[rows, :]
            if final_norm:
                y = y * lax.rsqrt(jnp.mean(y * y, axis=-1, keepdims=True) + EPS) * fg_ref[...]
            o_ref[rows, :] = y


def _ffn(x, g, sh, sc, gate, w_in, conv_w, conv_b, w_out, layer, final_g=None):
    t, d = x.shape
    f = w_out.shape[1]
    tm, tn = _pick_tm(t), 512
    nf = f // tn
    final_norm = final_g is not None
    fg = final_g if final_norm else g
    return pl.pallas_call(
        functools.partial(_ffn_kernel, tm=tm, final_norm=final_norm), grid=(t // tm, nf),
        in_specs=_halo_specs(t, tm, d) + [
            _vec_spec(d), _vec_spec(d), _vec_spec(d), _vec_spec(d), _vec_spec(d),
            pl.BlockSpec((None, d, tn), lambda i, j: (layer, 0, j)),
            pl.BlockSpec((None, d, tn), lambda i, j: (layer, 0, j + nf)),
            pl.BlockSpec((3, tn), lambda i, j: (0, j)),
            pl.BlockSpec((3, tn), lambda i, j: (0, j + nf)),
            pl.BlockSpec((1, tn), lambda i, j: (0, j)),
            pl.BlockSpec((1, tn), lambda i, j: (0, j + nf)),
            pl.BlockSpec((None, tn, d), lambda i, j: (layer, j, 0))],
        out_specs=pl.BlockSpec((tm, d), lambda i, j: (i, 0)),
        out_shape=jax.ShapeDtypeStruct((t, d), F32),
        scratch_shapes=[pltpu.VMEM((tm + 2 * HALO, d), BF16), pltpu.VMEM((tm, d), F32)],
        compiler_params=_cparams("parallel", "arbitrary"), name="conv_ffn",
    )(x, x, x, g, sh, sc, gate, fg, w_in, w_in, conv_w, conv_w, conv_b, conv_b, w_out)


def _outproj_kernel(a_ref, b_ref, x_ref, gate_ref, wa_ref, wb_ref, o_ref):
    mix = jnp.dot(a_ref[...], wa_ref[...], preferred_element_type=F32)
    mix = mix + jnp.dot(b_ref[...], wb_ref[...], preferred_element_type=F32)
    o_ref[...] = x_ref[...] + gate_ref[...] * mix


def _outproj(a, b, x, gate, wa, wb):
    t, d = x.shape
    tm = _pick_tm(t)
    return pl.pallas_call(
        _outproj_kernel, grid=(t // tm,),
        in_specs=[pl.BlockSpec((tm, MIX_W), lambda i: (i, 0)),
                  pl.BlockSpec((tm, MIX_W), lambda i: (i, 0)),
                  pl.BlockSpec((tm, d), lambda i: (i, 0)),
                  pl.BlockSpec((1, d), lambda i: (0, 0)),
                  pl.BlockSpec((MIX_W, d), lambda i: (0, 0)),
                  pl.BlockSpec((MIX_W, d), lambda i: (0, 0))],
        out_specs=pl.BlockSpec((tm, d), lambda i: (i, 0)),
        out_shape=jax.ShapeDtypeStruct((t, d), F32),
        compiler_params=_cparams("parallel"), name="out_proj",
    )(a, b, x, gate, wa, wb)


def _rope(x, cos, sin):
    lane = lax.broadcasted_iota(jnp.int32, x.shape, 1)
    up = pltpu.roll(x, 32, 1)
    dn = pltpu.roll(x, HEAD_DIM - 32, 1)
    rot = jnp.where((lane % 64) < 32, -dn, up)
    return x * cos + rot * sin


def _qk_kernel(q_ref, k_ref, v_ref, cos_ref, sin_ref, qg_ref, kg_ref, qo_ref, ko_ref, vo_ref, *, norm):
    cos, sin = cos_ref[...], sin_ref[...]

    def prep(xh, g, scale):
        if norm:
            xh = xh * lax.rsqrt(jnp.mean(xh * xh, axis=-1, keepdims=True) + EPS) * g
        xh = _rope(xh, cos, sin)
        return (xh * scale).astype(BF16) if scale != 1.0 else xh.astype(BF16)

    for h in range(N_Q_HEADS):
        sl = slice(h * HEAD_DIM, (h + 1) * HEAD_DIM)
        qo_ref[:, sl] = prep(q_ref[:, sl], qg_ref[...], Q_SCALE)
    for h in range(N_KV_HEADS):
        sl = slice(h * HEAD_DIM, (h + 1) * HEAD_DIM)
        ko_ref[:, sl] = prep(k_ref[:, sl], kg_ref[...], 1.0)
        vo_ref[:, h * V_AUG:h * V_AUG + HEAD_DIM] = v_ref[:, sl].astype(BF16)
        vo_ref[:, h * V_AUG + HEAD_DIM:(h + 1) * V_AUG] = jnp.ones((v_ref.shape[0], HEAD_DIM), BF16)


def _qk_prep(p, cos, sin, qg, kg, q_off, k_off, v_off, norm):
    t = p.shape[0]
    tm = 256
    return pl.pallas_call(
        functools.partial(_qk_kernel, norm=norm), grid=(t // tm,),
        in_specs=[pl.BlockSpec((tm, MIX_W), lambda i: (i, q_off // MIX_W)),
                  pl.BlockSpec((tm, KV_W), lambda i: (i, k_off // KV_W)),
                  pl.BlockSpec((tm, KV_W), lambda i: (i, v_off // KV_W)),
                  pl.BlockSpec((tm, HEAD_DIM), lambda i: (i, 0)),
                  pl.BlockSpec((tm, HEAD_DIM), lambda i: (i, 0)),
                  pl.BlockSpec((1, HEAD_DIM), lambda i: (0, 0)),
                  pl.BlockSpec((1, HEAD_DIM), lambda i: (0, 0))],
        out_specs=[pl.BlockSpec((tm, MIX_W), lambda i: (i, 0)),
                   pl.BlockSpec((tm, KV_W), lambda i: (i, 0)),
                   pl.BlockSpec((tm, N_KV_HEADS * V_AUG), lambda i: (i, 0))],
        out_shape=[jax.ShapeDtypeStruct((t, MIX_W), BF16),
                   jax.ShapeDtypeStruct((t, KV_W), BF16),
                   jax.ShapeDtypeStruct((t, N_KV_HEADS * V_AUG), BF16)],
        compiler_params=_cparams("parallel"), name="qk_prep",
    )(p, p, p, cos, sin, qg, kg)


def _flash_kernel(sink_ref, q_ref, k_ref, v_ref, o_ref, qs_ref, m_ref, acc_ref, *, tq, use_sink):
    g, ki = pl.program_id(0), pl.program_id(2)

    @pl.when(ki == 0)
    def _():
        for h in range(GQA_GROUP):
            qs_ref[h * tq:(h + 1) * tq, :] = q_ref[:, h * HEAD_DIM:(h + 1) * HEAD_DIM]
            if use_sink:
                m_ref[h * tq:(h + 1) * tq, :] = jnp.full((tq, 1), sink_ref[g * GQA_GROUP + h] * LOG2E, F32)
        acc_ref[:, :HEAD_DIM] = jnp.zeros((GQA_GROUP * tq, HEAD_DIM), F32)
        if use_sink:
            acc_ref[:, HEAD_DIM:] = jnp.ones((GQA_GROUP * tq, HEAD_DIM), F32)
        else:
            m_ref[...] = jnp.full_like(m_ref, -jnp.inf)
            acc_ref[:, HEAD_DIM:] = jnp.zeros((GQA_GROUP * tq, HEAD_DIM), F32)

    k, v = k_ref[...], v_ref[...]

    def scores(h):
        return lax.dot_general(qs_ref[h * tq:(h + 1) * tq, :], k, _NT, preferred_element_type=F32)

    def weigh(h, s):
        rows = slice(h * tq, (h + 1) * tq)
        m_prev = m_ref[rows, :]
        m_new = jnp.maximum(m_prev, jnp.max(s, axis=-1, keepdims=True))
        m_ref[rows, :] = m_new
        p = jnp.exp2(s - m_new).astype(BF16)
        return jnp.exp2(m_prev - m_new), jnp.dot(p, v, preferred_element_type=F32)

    s_next = scores(0)
    upd = []
    for h in range(GQA_GROUP):
        s_cur = s_next
        if h + 1 < GQA_GROUP:
            s_next = scores(h + 1)
        upd.append(weigh(h, s_cur))
    for h, (alpha, pv) in enumerate(upd):
        rows = slice(h * tq, (h + 1) * tq)
        acc_ref[rows, :] = alpha * acc_ref[rows, :] + pv

    @pl.when(ki == pl.num_programs(2) - 1)
    def _():
        out = acc_ref[:, :HEAD_DIM] / acc_ref[:, HEAD_DIM:]
        for h in range(GQA_GROUP):
            o_ref[:, h * HEAD_DIM:(h + 1) * HEAD_DIM] = out[h * tq:(h + 1) * tq, :].astype(BF16)


def _pick_tk(tk_total):
    for c in (3328, 1280, 1024, 512, 256):
        if tk_total % c == 0:
            return c
    return 128


def _dense_gqa(q, k, v, sink=None):
    tq_total, tk_total = q.shape[0], k.shape[0]
    tq = 512 if tq_total % 512 == 0 else 256
    tk = _pick_tk(tk_total)
    use_sink = sink is not None
    sink_arr = sink if use_sink else jnp.zeros((N_Q_HEADS,), F32)
    gw = GQA_GROUP * HEAD_DIM
    return pl.pallas_call(
        functools.partial(_flash_kernel, tq=tq, use_sink=use_sink),
        grid=(N_KV_HEADS, tq_total // tq, tk_total // tk),
        in_specs=[pl.BlockSpec(memory_space=pltpu.SMEM),
                  pl.BlockSpec((tq, gw), lambda g, qi, ki: (qi, g)),
                  pl.BlockSpec((tk, HEAD_DIM), lambda g, qi, ki: (ki, g)),
                  pl.BlockSpec((tk, V_AUG), lambda g, qi, ki: (ki, g))],
        out_specs=pl.BlockSpec((tq, gw), lambda g, qi, ki: (qi, g)),
        out_shape=jax.ShapeDtypeStruct((tq_total, MIX_W), BF16),
        scratch_shapes=[pltpu.VMEM((GQA_GROUP * tq, HEAD_DIM), BF16),
                        pltpu.VMEM((GQA_GROUP * tq, 1), F32),
                        pltpu.VMEM((GQA_GROUP * tq, V_AUG), F32)],
        compiler_params=_cparams("parallel", "parallel", "arbitrary"), name="dense_gqa",
    )(sink_arr, q, k, v)


def _window_kernel(sink_ref, q_ref, kp_ref, k0_ref, kx_ref, kc_ref, vp_ref, v0_ref, vx_ref, vc_ref, o_ref, *, tq):
    g, n = pl.program_id(0), pl.program_id(1)
    nq = pl.num_programs(1)
    band = tq + 2 * BLK
    kcat = jnp.concatenate([kp_ref[...], k0_ref[...], kx_ref[...], kc_ref[...]], axis=0)
    vcat = jnp.concatenate([vp_ref[...], v0_ref[...], vx_ref[...], vc_ref[...]], axis=0)
    shape = (tq, kcat.shape[0])
    row = lax.broadcasted_iota(jnp.int32, shape, 0)
    col = lax.broadcasted_iota(jnp.int32, shape, 1)
    rel = col - BLK - row
    valid = (jnp.abs(rel) <= BLK) & ((col >= BLK) | (n > 0)) & ((col < BLK + tq) | (n < nq - 1))
    valid = valid | (col >= band)

    def scores(h):
        return lax.dot_general(q_ref[:, h * HEAD_DIM:(h + 1) * HEAD_DIM], kcat, _NT, preferred_element_type=F32)

    s_next = scores(0)
    for h in range(GQA_GROUP):
        s = jnp.where(valid, s_next, -jnp.inf)
        if h + 1 < GQA_GROUP:
            s_next = scores(h + 1)
        sink = sink_ref[g * GQA_GROUP + h] * LOG2E
        m = jnp.maximum(jnp.max(s, axis=-1, keepdims=True), sink)
        p = jnp.exp2(s - m).astype(BF16)
        pv = jnp.dot(p, vcat, preferred_element_type=F32)
        out = pv[:, :HEAD_DIM] / (pv[:, HEAD_DIM:] + jnp.exp2(sink - m))
        o_ref[:, h * HEAD_DIM:(h + 1) * HEAD_DIM] = out.astype(BF16)


def _window_gqa(q, k, v, kc, vc, sink):
    t = q.shape[0]
    tq = 256 if t % 256 == 0 else BLK
    per = tq // BLK
    nb = t // BLK
    tc = kc.shape[0]
    gw = GQA_GROUP * HEAD_DIM
    prev = lambda g, n: (jnp.maximum(n * per - 1, 0), g)
    cur = lambda g, n: (n, g)
    nxt = lambda g, n: (jnp.minimum((n + 1) * per, nb - 1), g)
    ctx = lambda g, n: (0, g)
    kspecs = [pl.BlockSpec((BLK, HEAD_DIM), prev), pl.BlockSpec((tq, HEAD_DIM), cur),
              pl.BlockSpec((BLK, HEAD_DIM), nxt), pl.BlockSpec((tc, HEAD_DIM), ctx)]
    vspecs = [pl.BlockSpec((BLK, V_AUG), prev), pl.BlockSpec((tq, V_AUG), cur),
              pl.BlockSpec((BLK, V_AUG), nxt), pl.BlockSpec((tc, V_AUG), ctx)]
    return pl.pallas_call(
        functools.partial(_window_kernel, tq=tq), grid=(N_KV_HEADS, t // tq),
        in_specs=[pl.BlockSpec(memory_space=pltpu.SMEM), pl.BlockSpec((tq, gw), lambda g, n: (n, g))]
        + kspecs + vspecs,
        out_specs=pl.BlockSpec((tq, gw), lambda g, n: (n, g)),
        out_shape=jax.ShapeDtypeStruct((t, MIX_W), BF16),
        compiler_params=_cparams("parallel", "parallel"), name="window_gqa",
    )(sink, q, k, k, k, kc, v, v, v, vc)


def _sgu_kernel(u_ref, z_ref, ng_ref, ws_ref, bs_ref, o_ref, *, chunks):
    for c in range(chunks):
        rows = slice(c * BLK, (c + 1) * BLK)
        for g in range(SGU_GROUPS):
            cols = slice(g * 128, (g + 1) * 128)
            z = _gelu_tanh(z_ref[rows, cols])
            mu = jnp.mean(z, axis=-1, keepdims=True)
            zc = z - mu
            var = jnp.mean(zc * zc, axis=-1, keepdims=True)
            vn = zc * lax.rsqrt(var + EPS) * ng_ref[:, cols]
            vm = jnp.dot(ws_ref[g], vn.astype(BF16), preferred_element_type=F32) + bs_ref[g]
            o_ref[rows, cols] = (_gelu_tanh(u_ref[rows, cols]) * vm).astype(BF16)


def _sgu(p, norm_g, w_s, b_s):
    t = p.shape[0]
    chunks = 2
    tm = chunks * BLK
    bs = jnp.broadcast_to(b_s[:, :, None], (SGU_GROUPS, BLK, 128))
    return pl.pallas_call(
        functools.partial(_sgu_kernel, chunks=chunks), grid=(t // tm,),
        in_specs=[pl.BlockSpec((tm, MIX_W), lambda i: (i, EV_U // MIX_W)),
                  pl.BlockSpec((tm, MIX_W), lambda i: (i, EV_Z // MIX_W)),
                  pl.BlockSpec((1, MIX_W), lambda i: (0, 0)),
                  pl.BlockSpec((SGU_GROUPS, BLK, BLK), lambda i: (0, 0, 0)),
                  pl.BlockSpec((SGU_GROUPS, BLK, 128), lambda i: (0, 0, 0))],
        out_specs=pl.BlockSpec((tm, MIX_W), lambda i: (i, 0)),
        out_shape=jax.ShapeDtypeStruct((t, MIX_W), BF16),
        compiler_params=_cparams("parallel"), name="chunk_sgu",
    )(p, p, norm_g.reshape(1, MIX_W), w_s.astype(BF16), bs)


def _split3(x):
    hi = x.astype(BF16)
    r1 = x - hi.astype(F32)
    mid = r1.astype(BF16)
    lo = (r1 - mid.astype(F32)).astype(BF16)
    return hi, mid, lo


def _seg_sum(x, e):
    parts = []
    for b in range(x.shape[1] // 128):
        hi, mid, lo = _split3(x[:, b * 128:(b + 1) * 128])
        parts.append(jnp.dot(hi, e, preferred_element_type=F32) + jnp.dot(mid, e, preferred_element_type=F32)
                     + jnp.dot(lo, e, preferred_element_type=F32))
    return jnp.concatenate(parts, axis=1)


def _run_sum(tri, x):
    hi, mid, lo = _split3(x)
    return (jnp.dot(tri, hi, preferred_element_type=F32) + jnp.dot(tri, mid, preferred_element_type=F32)
            + jnp.dot(tri, lo, preferred_element_type=F32))


def _feat_kernel(r_ref, k_ref, v_ref, gl_ref, wl_ref, al_ref, g2_ref, w2_ref, a2_ref, w0_ref, a0_ref,
                 kk_ref, ka_ref, rk_ref, e_ref, tri_ref, ld_ref, cum_ref, kd_ref, ad_ref, kko_ref, bon_ref, go_ref):
    e = e_ref[...]
    r, k, v = r_ref[...], k_ref[...], v_ref[...]
    go_ref[...] = jnp.dot(_sigmoid(gl_ref[...]), g2_ref[...], preferred_element_type=F32)
    lw = w0_ref[...] + jnp.dot(jnp.tanh(wl_ref[...]), w2_ref[...], preferred_element_type=F32)
    nl = -lw
    softplus = jnp.maximum(nl, 0.0) + jnp.log(1.0 + jnp.exp(-jnp.abs(nl)))
    ld = -jnp.exp(-softplus - 0.5)
    ld_ref[...] = ld
    cum_ref[:, :MIX_W] = _run_sum(tri_ref[0], ld[:, :MIX_W])
    cum_ref[:, MIX_W:] = _run_sum(tri_ref[1], ld[:, MIX_W:])
    a = _sigmoid(a0_ref[...] + jnp.dot(al_ref[...], a2_ref[...], preferred_element_type=F32))
    ad_ref[...] = a
    kk = k * kk_ref[...]
    nrm = jnp.maximum(jnp.sqrt(_seg_sum(kk * kk, e)), 1e-12)
    kko_ref[...] = kk / nrm
    ka = ka_ref[...]
    kd0 = k * (1.0 + (a[:, :MIX_W] - 1.0) * ka)
    kd1 = k * (1.0 + (a[:, MIX_W:] - 1.0) * ka)
    kd_ref[:, :MIX_W] = kd0
    kd_ref[:, MIX_W:] = kd1
    bon_ref[...] = _seg_sum(r * (kd0 + kd1) * rk_ref[...], e) * v


def _rwkv_features(zs, g2p, w2b, a2b, w0, a0, k_k, k_a, r_k, e):
    t = zs.shape[0]
    tm = 256
    col = lambda w, off: pl.BlockSpec((tm, w), lambda i: (i, off // w))
    full = lambda a: pl.BlockSpec(a.shape, lambda i: (0,) * a.ndim)
    wide = lambda w: pl.BlockSpec((tm, w), lambda i: (i, 0))
    row = jnp.arange(tm)
    same_chunk = (row[:, None] // SCAN_CHUNK) == (row[None, :] // SCAN_CHUNK)
    tri = jnp.stack([same_chunk & (row[None, :] <= row[:, None]),
                     same_chunk & (row[None, :] >= row[:, None])]).astype(BF16)
    params = [g2p, w2b, a2b, w0, a0, k_k, k_a, r_k, e, tri]
    return pl.pallas_call(
        _feat_kernel, grid=(t // tm,),
        in_specs=[col(MIX_W, OD_R), col(MIX_W, OD_K), col(MIX_W, OD_V), col(D_GATE_PAD, OD_G),
                  col(128, OD_W), col(128, OD_A)] + [full(a) for a in params],
        out_specs=[wide(2 * MIX_W)] * 4 + [wide(MIX_W)] * 3,
        out_shape=[jax.ShapeDtypeStruct((t, 2 * MIX_W), F32)] * 4 + [jax.ShapeDtypeStruct((t, MIX_W), F32)] * 3,
        compiler_params=_cparams("parallel"), name="rwkv_features",
    )(zs, zs, zs, zs, zs, zs, *params)


def _bdot(a, b, dims=(((1,), (0,)), ((), ()))):
    return lax.dot_general(a.astype(BF16), b.astype(BF16), dims, preferred_element_type=F32)


def _scan_masks():
    L = SCAN_CHUNK
    r = jnp.arange(2 * L)[:, None]
    c = jnp.arange(2 * L)[None, :]
    same = (r // L) == (c // L)
    out = []
    for reverse in (False, True):
        before = (c > r) if reverse else (c < r)
        out += [same & before, same & (before | (c == r))]
        b = 1
        while b < L:
            late_r, late_c = (r // b) % 2 == 1, (c // b) % 2 == 1
            couple = (~late_r & late_c) if reverse else (late_r & ~late_c)
            out.append(((r // (2 * b)) == (c // (2 * b))) & couple)
            b *= 2
    return jnp.stack(out).astype(F32), same.astype(F32)


def _chunk_group(streams, mask_ref, nlv, same, eye, m0, m1):
    L = SCAN_CHUNK
    n = len(streams)
    pre = []
    for ld, cum, kd, ad, r, v, kk, s_prev, d in streams:
        g_inc = jnp.exp(cum)
        g_exc = jnp.exp(cum - ld)
        g_inv = jnp.exp(-cum)
        last = 0 if d else L - 1
        g_tot = g_inc[last:last + 1, :]
        at = -kk * g_exc
        rt = r * g_inc
        bt = (kk * ad * g_inv).astype(BF16)
        kt = (kd * g_inv).astype(BF16)
        lar = jnp.concatenate([at * m0, at * m1, rt * m0, rt * m1], axis=0).astype(BF16)
        vbd = jnp.concatenate([v * m0, v * m1], axis=0).astype(BF16)
        pre.append((lar, bt, kt, vbd, v.astype(BF16), g_tot, s_prev, d * nlv))
    gram = [_bdot(lar, jnp.concatenate([bt, bt, kt, kt], axis=0), _NT) for lar, bt, kt, _, _, _, _, _ in pre]
    xs = [_bdot(p[0], p[6], _NT) for p in pre]
    n_ab, m_ak, m_r = [], [], []
    for g, p in zip(gram, pre):
        strict, incl = mask_ref[p[7]] > 0, mask_ref[p[7] + 1] > 0
        n_ab.append(jnp.where(strict, g[:2 * L, :2 * L], 0.0))
        m_ak.append(jnp.where(strict, g[:2 * L, 2 * L:], 0.0).astype(BF16))
        m_r.append(jnp.concatenate([jnp.where(incl, g[2 * L:, :2 * L], 0.0),
                                    jnp.where(incl, g[2 * L:, 2 * L:], 0.0)], axis=1).astype(BF16))
    x = [xs[i][:2 * L] + _bdot(m_ak[i], pre[i][3]) for i in range(n)]
    tinv = [eye + n_ab[i] * mask_ref[pre[i][7] + 2] for i in range(n)]
    for q in range(3, nlv):
        tb = [t.astype(BF16) for t in tinv]
        half = [_bdot(tb[i], n_ab[i] * mask_ref[pre[i][7] + q]) for i in range(n)]
        tinv = [tinv[i] + _bdot(half[i], tb[i]) for i in range(n)]
    ubd = [_bdot(tinv[i], x[i]) for i in range(n)]
    ybd = [xs[i][2 * L:] + _bdot(m_r[i], jnp.concatenate([ubd[i].astype(BF16), pre[i][3]], axis=0)) for i in range(n)]
    out = []
    for i in range(n):
        _, bt, kt, _, vb, g_tot, s_prev, _ = pre[i]
        u = (ubd[i][:L, :] + ubd[i][L:, :]).astype(BF16)
        upd = _bdot(jnp.concatenate([u, vb], axis=0), jnp.concatenate([bt, kt], axis=0), _TN)
        out.append((ybd[i][:L, :] + ybd[i][L:, :], (s_prev + upd * same) * g_tot))
    return out


def _scan_kernel(ldf_ref, cumf_ref, kdf_ref, adf_ref, rf_ref, vf_ref, kkf_ref,
                 ldb_ref, cumb_ref, kdb_ref, adb_ref, rb_ref, vb_ref, kkb_ref,
                 s0f_ref, s0b_ref, mask_ref, same_ref, eye_ref,
                 yf_ref, yb_ref, sff_ref, sfb_ref, sf_ref, sb_ref):
    L = SCAN_CHUNK
    c = pl.program_id(0)

    @pl.when(c == 0)
    def _():
        sf_ref[...] = s0f_ref[...]
        sb_ref[...] = s0b_ref[...]

    lane = lax.broadcasted_iota(jnp.int32, (L, 2 * RWKV_HEAD), 1)
    m0 = (lane < RWKV_HEAD).astype(F32)
    m1 = 1.0 - m0
    nlv = mask_ref.shape[0] // 2
    same, eye = same_ref[...], eye_ref[...]
    dirs = ((ldf_ref, cumf_ref, kdf_ref, adf_ref, rf_ref, vf_ref, kkf_ref, sf_ref, yf_ref, 0),
            (ldb_ref, cumb_ref, kdb_ref, adb_ref, rb_ref, vb_ref, kkb_ref, sb_ref, yb_ref, 1))
    for p0 in range(0, RWKV_PAIRS, SCAN_GROUP_PAIRS):
        streams, dests = [], []
        for p in range(p0, p0 + SCAN_GROUP_PAIRS):
            cols = slice(p * 128, (p + 1) * 128)
            for ld_ref, cum_ref, kd_ref, ad_ref, r_ref, v_ref, kk_ref, s_ref, y_ref, d in dirs:
                streams.append((ld_ref[:, cols], cum_ref[:, cols], kd_ref[:, cols], ad_ref[:, cols],
                                r_ref[:, cols], v_ref[:, cols], kk_ref[:, cols], s_ref[p], d))
                dests.append((y_ref, s_ref, p, cols))
        for (y_ref, s_ref, p, cols), (y, s_new) in zip(dests, _chunk_group(streams, mask_ref, nlv, same, eye, m0, m1)):
            y_ref[:, cols] = y
            s_ref[p] = s_new

    @pl.when(c == pl.num_programs(0) - 1)
    def _():
        sff_ref[...] = sf_ref[...]
        sfb_ref[...] = sb_ref[...]


def _rwkv_scan(ld, cum, kd, ad, zs, kk, s0f, s0b):
    t = zs.shape[0]
    L = SCAN_CHUNK
    nc = t // L
    masks, same = _scan_masks()
    eye = jnp.eye(2 * L, dtype=F32)
    specs = []
    for d, cidx in ((0, lambda c: c), (1, lambda c: nc - 1 - c)):
        dirblk = pl.BlockSpec((L, MIX_W), lambda c, cidx=cidx, d=d: (cidx(c), d))
        specs += [dirblk, dirblk, dirblk, dirblk,
                  pl.BlockSpec((L, MIX_W), lambda c, cidx=cidx: (cidx(c), OD_R // MIX_W)),
                  pl.BlockSpec((L, MIX_W), lambda c, cidx=cidx: (cidx(c), OD_V // MIX_W)),
                  pl.BlockSpec((L, MIX_W), lambda c, cidx=cidx: (cidx(c), 0))]
    sblk = pl.BlockSpec((RWKV_PAIRS, 128, 128), lambda c: (0, 0, 0))
    const = lambda a: pl.BlockSpec(a.shape, lambda c: (0,) * a.ndim)
    feats = (ld, cum, kd, ad, zs, zs, kk)
    return pl.pallas_call(
        _scan_kernel, grid=(nc,),
        in_specs=specs + [sblk, sblk, const(masks), const(same), const(eye)],
        out_specs=[pl.BlockSpec((L, MIX_W), lambda c: (c, 0)),
                   pl.BlockSpec((L, MIX_W), lambda c: (nc - 1 - c, 0)), sblk, sblk],
        out_shape=[jax.ShapeDtypeStruct((t, MIX_W), F32)] * 2 + [jax.ShapeDtypeStruct((RWKV_PAIRS, 128, 128), F32)] * 2,
        scratch_shapes=[pltpu.VMEM((RWKV_PAIRS, 128, 128), F32)] * 2,
        compiler_params=_cparams("arbitrary"), name="rwkv_scan",
    )(*feats, *feats, s0f, s0b, masks, same, eye)


def _readout_kernel(yf_ref, yb_ref, bon_ref, g_ref, lw_ref, lb_ref, e_ref, o_ref):
    e = e_ref[...]
    y = yf_ref[...] + yb_ref[...]
    mu = _seg_sum(y, e) * (1.0 / RWKV_HEAD)
    yc = y - mu
    var = _seg_sum(yc * yc, e) * (1.0 / RWKV_HEAD)
    yn = yc * lax.rsqrt(var + GN_EPS) * lw_ref[...] + lb_ref[...]
    o_ref[...] = ((yn + bon_ref[...]) * g_ref[...]).astype(BF16)


def _rwkv_readout(yf, yb, bonus, g, ln_w, ln_b, e):
    t = yf.shape[0]
    tm = 256
    wide = pl.BlockSpec((tm, MIX_W), lambda i: (i, 0))
    vec = pl.BlockSpec((1, MIX_W), lambda i: (0, 0))
    return pl.pallas_call(
        _readout_kernel, grid=(t // tm,),
        in_specs=[wide, wide, wide, wide, vec, vec, pl.BlockSpec((128, 128), lambda i: (0, 0))],
        out_specs=wide, out_shape=jax.ShapeDtypeStruct((t, MIX_W), BF16),
        compiler_params=_cparams("parallel"), name="rwkv_readout",
    )(yf, yb, bonus, g, ln_w, ln_b, e)


def _rope_tables(rows):
    half = HEAD_DIM // 2
    inv = ROPE_THETA ** (-jnp.arange(0, half, 2, dtype=F32) / half)
    row = jnp.repeat(jnp.arange(rows, dtype=F32), GRID_W)
    col = jnp.tile(jnp.arange(GRID_W, dtype=F32), rows)
    ang_r = row[:, None] * inv[None, :]
    ang_c = col[:, None] * inv[None, :]
    ang = jnp.concatenate([ang_r, ang_r, ang_c, ang_c], axis=-1)
    return jnp.cos(ang), jnp.sin(ang)


def _even_layer(x, ctx, mx, mc, n1g, w_in, w_out, qg, kg, vng, w_s, b_s, tabs, with_ctx):
    o_k = N_Q_HEADS * HEAD_DIM
    o_v = o_k + KV_W
    o_u = o_v + KV_W
    o_z = o_u + MIX_W
    wp = jnp.concatenate([w_in[:, :o_k], w_in[:, o_u:o_z], w_in[:, o_z:], w_in[:, o_k:o_v], w_in[:, o_v:o_u]],
                         axis=1).astype(BF16)
    wa, wb = w_out[:MIX_W].astype(BF16), w_out[MIX_W:].astype(BF16)
    qg, kg = qg.reshape(1, HEAD_DIM), kg.reshape(1, HEAD_DIM)
    (cos_x, sin_x), (cos_c, sin_c) = tabs
    px = _proj(x, n1g, mx[0], mx[1], wp)
    pc = _proj(ctx, n1g, mc[0], mc[1], wp)
    qx, kx, vx = _qk_prep(px, cos_x, sin_x, qg, kg, EV_Q, EV_K, EV_V, True)
    qc, kc, vc = _qk_prep(pc, cos_c, sin_c, qg, kg, EV_Q, EV_K, EV_V, True)
    ax = _dense_gqa(qx, jnp.concatenate([kx, kc], axis=0), jnp.concatenate([vx, vc], axis=0))
    bx = _sgu(px, vng, w_s, b_s)
    x = _outproj(ax, bx, x, mx[2], wa, wb)
    if with_ctx:
        ac = _dense_gqa(qc, kc, vc)
        bc = _sgu(pc, vng, w_s, b_s)
        ctx = _outproj(ac, bc, ctx, mc[2], wa, wb)
    return x, ctx


def _odd_layer(x, ctx, mx, mc, n1g, w_in, w_out, sink, shift_mu, w0, w2, a0, a2, g2, k_k, k_a, r_k, ln_w, ln_b,
               tabs, with_ctx):
    o_k = N_Q_HEADS * HEAD_DIM
    o_v = o_k + KV_W
    c_in = o_v + KV_W
    o_g = 3 * MIX_W
    o_w = o_g + D_GATE_LORA
    o_a = o_w + 2 * D_LORA
    d = w_in.shape[0]
    wr = w_in[:, c_in:]
    gpad = jnp.zeros((d, D_GATE_PAD - D_GATE_LORA), w_in.dtype)
    wp = jnp.concatenate([w_in[:, :o_k], wr[:, :o_g], w_in[:, o_k:o_v], w_in[:, o_v:c_in],
                          wr[:, o_g:o_w], gpad, wr[:, o_w:o_a], wr[:, o_a:]], axis=1).astype(BF16)
    taps = jnp.stack([shift_mu[0], 1.0 - shift_mu[0] - shift_mu[1], shift_mu[1]], axis=0)
    ident = jnp.tile(jnp.array([[0.0], [1.0], [0.0]], F32), (1, 1))
    cw = jnp.concatenate([jnp.tile(ident, (1, MIX_W)), taps[:, :o_g], jnp.tile(ident, (1, 2 * KV_W)),
                          taps[:, o_g:o_w], jnp.tile(ident, (1, D_GATE_PAD - D_GATE_LORA)),
                          taps[:, o_w:o_a], taps[:, o_a:]], axis=1)
    wa, wb = w_out[:MIX_W].astype(BF16), w_out[MIX_W:].astype(BF16)
    g2p = jnp.concatenate([g2, jnp.zeros((D_GATE_PAD - D_GATE_LORA, MIX_W), g2.dtype)], axis=0)
    zl = jnp.zeros((D_LORA, MIX_W), F32)
    w2b = jnp.concatenate([jnp.concatenate([w2[0], zl], axis=1), jnp.concatenate([zl, w2[1]], axis=1)], axis=0)
    a2b = jnp.concatenate([jnp.concatenate([a2[0], zl], axis=1), jnp.concatenate([zl, a2[1]], axis=1)], axis=0)
    w0f, a0f = w0.reshape(1, 2 * MIX_W), a0.reshape(1, 2 * MIX_W)
    kkv, kav, rkv = k_k.reshape(1, MIX_W), k_a.reshape(1, MIX_W), r_k.reshape(1, MIX_W)
    lnw, lnb = ln_w.reshape(1, MIX_W), ln_b.reshape(1, MIX_W)
    lane = jnp.arange(128)
    e = (lane[:, None] // RWKV_HEAD == lane[None, :] // RWKV_HEAD).astype(BF16)
    ones_g = jnp.ones((1, HEAD_DIM), F32)
    (cos_x, sin_x), (cos_c, sin_c) = tabs

    plain = ((OD_Q, OD_R), (OD_AK, OD_G))
    px = _proj(x, n1g, mx[0], mx[1], wp, cw, plain)
    pc = _proj(ctx, n1g, mc[0], mc[1], wp, cw, plain)
    qx, kx, vx = _qk_prep(px, cos_x, sin_x, ones_g, ones_g, OD_Q, OD_AK, OD_AV, False)
    qc, kc, vc = _qk_prep(pc, cos_c, sin_c, ones_g, ones_g, OD_Q, OD_AK, OD_AV, False)
    cx = _window_gqa(qx, kx, vx, kc, vc, sink)

    fparams = (g2p, w2b, a2b, w0f, a0f, kkv, kav, rkv, e)
    ld_x, cum_x, kd_x, ad_x, kk_x, bon_x, g_x = _rwkv_features(px, *fparams)
    ld_c, cum_c, kd_c, ad_c, kk_c, bon_c, g_c = _rwkv_features(pc, *fparams)
    s0 = jnp.zeros((RWKV_PAIRS, 128, 128), F32)
    y_cf, y_cb, s_cf, s_cb = _rwkv_scan(ld_c, cum_c, kd_c, ad_c, pc, kk_c, s0, s0)
    y_xf, y_xb, _, _ = _rwkv_scan(ld_x, cum_x, kd_x, ad_x, px, kk_x, s_cf, s_cb)
    dx = _rwkv_readout(y_xf, y_xb, bon_x, g_x, lnw, lnb, e)
    x = _outproj(cx, dx, x, mx[2], wa, wb)
    if with_ctx:
        cc = _dense_gqa(qc, kc, vc, sink)
        dc = _rwkv_readout(y_cf, y_cb, bon_c, g_c, lnw, lnb, e)
        ctx = _outproj(cc, dc, ctx, mc[2], wa, wb)
    return x, ctx


def kernel(x, c, ctx, c_ctx, ada_w, ada_b, norm1_g, norm2_g, ffn_w_in, ffn_conv_w, ffn_conv_b, ffn_w_out,
           ev_w_in, ev_w_out, a_q_norm_g, a_k_norm_g, b_v_norm_g, b_spatial_w, b_spatial_b,
           od_w_in, od_w_out, c_sink, d_shift_mu, d_w0, d_w2, d_a0, d_a2, d_g2, d_k_k, d_k_a, d_r_k,
           d_ln_w, d_ln_b, final_norm_g):
    bsz, t, d = x.shape
    assert bsz == 1 and d == D_MODEL and ada_w.shape[0] >= 1
    tc = ctx.shape[1]
    depth = ada_w.shape[0]
    xs, cs = x.reshape(t, d), ctx.reshape(tc, d)
    tabs = (_rope_tables(t // GRID_W), (jnp.ones((tc, HEAD_DIM), F32), jnp.zeros((tc, HEAD_DIM), F32)))
    sv = jnp.stack([jax.nn.silu(c[0]), jax.nn.silu(c_ctx)], axis=1)
    mods = _ada_mod(sv, ada_w, ada_b)
    fw_in, fw_out = ffn_w_in.astype(BF16), ffn_w_out.astype(BF16)
    for l in range(depth):
        with_ctx = l < depth - 1
        mx = [mods[l, 0, k * d:(k + 1) * d].reshape(1, d) for k in range(6)]
        mc = [mods[l, 1, k * d:(k + 1) * d].reshape(1, d) for k in range(6)]
        n1g, n2g = norm1_g[l].reshape(1, d), norm2_g[l].reshape(1, d)
        i = l // 2
        if l % 2 == 0:
            xs, cs = _even_layer(xs, cs, mx, mc, n1g, ev_w_in[i], ev_w_out[i], a_q_norm_g[i], a_k_norm_g[i],
                                 b_v_norm_g[i], b_spatial_w[i], b_spatial_b[i], tabs, with_ctx)
        else:
            xs, cs = _odd_layer(xs, cs, mx, mc, n1g, od_w_in[i], od_w_out[i], c_sink[i], d_shift_mu[i], d_w0[i],
                                d_w2[i], d_a0[i], d_a2[i], d_g2[i], d_k_k[i], d_k_a[i], d_r_k[i], d_ln_w[i],
                                d_ln_b[i], tabs, with_ctx)
        fcb = ffn_conv_b[l].reshape(1, 2 * D_FF)
        final_g = final_norm_g.reshape(1, d) if l == depth - 1 else None
        xs = _ffn(xs, n2g, mx[3], mx[4], mx[5], fw_in, ffn_conv_w[l], fcb, fw_out, l, final_g)
        if with_ctx:
            cs = _ffn(cs, n2g, mc[3], mc[4], mc[5], fw_in, ffn_conv_w[l], fcb, fw_out, l)
    return xs.reshape(1, t, d)
```
